```python
import jax, jax.numpy as jnp
from jax import lax
import numpy as np

D_MODEL = 1024
BATCH = 16
SEQ = 4096
DEPTH = 1
DEC_BATCH = 4
DEC_SEQ = 8192
PAST_LEN = 128

POOL_WINDOWS = (2, 4, 8, 16)
POOL_GROUPS = 4
POOL_GROUP_WIDTH = D_MODEL // 8
POOL_WIDTH = POOL_GROUPS * POOL_GROUP_WIDTH
HG_HEADS = 8
HG_HEAD_DIM = D_MODEL // HG_HEADS
HG_DIM = HG_HEADS * HG_HEAD_DIM
CHUNK = 64
PEER_HEADS = 8
PEER_QUERY_DIM = 256
PEER_HALF = PEER_QUERY_DIM // 2
N_KEYS = 128
N_EXPERTS = N_KEYS * N_KEYS
PEER_TOPK = 16
PEER_BLOCK = 128
NORM_EPS = 1e-6

kernel_name = "hybrid_pool_hgrn2_peer_encoder"


def _rms_norm(x, w):
    xf = x.astype(jnp.float32)
    y = xf * lax.rsqrt(jnp.mean(xf * xf, axis=-1, keepdims=True) + NORM_EPS)
    return (y * w.astype(jnp.float32)).astype(x.dtype)


def _multiscale_pool(a):
    bsz, seq, ch = a.shape
    af = a.astype(jnp.float32)
    cs = jnp.concatenate([jnp.zeros((bsz, 1, ch), jnp.float32), jnp.cumsum(af, axis=1)], axis=1)
    t = jnp.arange(seq)
    outs = []
    for g, w in enumerate(POOL_WINDOWS):
        lo = jnp.clip(t - w // 2, 0, seq)
        hi = jnp.clip(t + w // 2, 0, seq)
        sl = slice(g * POOL_GROUP_WIDTH, (g + 1) * POOL_GROUP_WIDTH)
        csg = cs[..., sl]
        cnt = (hi - lo).astype(jnp.float32)[None, :, None]
        outs.append((csg[:, hi] - csg[:, lo]) / cnt - af[..., sl])
    return jnp.stack(outs, axis=2)


def _gla_scan(q, k, v, logf):
    bsz, seq, nh, dk = q.shape
    dv = v.shape[-1]
    nc = seq // CHUNK

    def to_chunks(t):
        return t.reshape(bsz, nc, CHUNK, nh, t.shape[-1]).transpose(1, 0, 3, 2, 4)

    mask = jnp.tril(jnp.ones((CHUNK, CHUNK), bool))[None, None, :, :, None]

    def step(state, inp):
        qc, kc, vc, gc = inp
        b = jnp.cumsum(gc, axis=2)
        rel = jnp.where(mask, b[:, :, :, None, :] - b[:, :, None, :, :], -jnp.inf)
        att = jnp.einsum('bhtsd,bhsd->bhts', qc[:, :, :, None, :] * jnp.exp(rel), kc)
        o = (jnp.einsum('bhts,bhsv->bhtv', att, vc)
             + jnp.einsum('bhtd,bhdv->bhtv', qc * jnp.exp(b), state))
        b_last = b[:, :, -1:, :]
        state = (jnp.exp(b_last[:, :, 0, :])[..., None] * state
                 + jnp.einsum('bhsd,bhsv->bhdv', kc * jnp.exp(b_last - b), vc))
        return state, o

    s0 = jnp.zeros((bsz, nh, dk, dv), jnp.float32)
    _, o = lax.scan(step, s0, (to_chunks(q), to_chunks(k), to_chunks(v), to_chunks(logf)))
    return o.transpose(1, 0, 3, 2, 4).reshape(bsz, seq, nh, dv)


def _hgrn_gates(fz, lb):
    fz = fz.astype(jnp.float32)
    logf = jnp.log(lb + (1.0 - lb) * jax.nn.sigmoid(fz))
    k = (1.0 - lb) * jax.nn.sigmoid(-fz)
    return k, logf


def _mixer(h, layer, w_in, pool_w, pool_scale, lb_fwd, lb_bwd, hg_norm_w, w_pa, w_pb, w_out):
    bsz, seq, _ = h.shape
    z = h @ w_in
    widths = (POOL_WIDTH, HG_DIM, HG_DIM, HG_DIM, HG_DIM, HG_DIM, D_MODEL, D_MODEL)
    cuts = [int(c) for c in np.cumsum(widths)[:-1]]
    a, qz, ffz, fbz, iz, ogz, ga, gb = jnp.split(z, cuts, axis=-1)

    pooled = _multiscale_pool(a)
    pa = jnp.einsum('blgc,gcd->blgd', pooled, pool_w.astype(jnp.float32)).reshape(bsz, seq, POOL_WIDTH)
    pa = (pa * pool_scale.astype(jnp.float32)).astype(h.dtype) @ w_pa

    def heads(t):
        return t.astype(jnp.float32).reshape(bsz, seq, HG_HEADS, HG_HEAD_DIM)
    q = heads(jax.nn.silu(qz))
    v = heads(iz)
    lbf = jnp.cumsum(jax.nn.softmax(lb_fwd.astype(jnp.float32), axis=0), axis=0)[layer]
    lbb = jnp.cumsum(jax.nn.softmax(lb_bwd.astype(jnp.float32), axis=0), axis=0)[layer]
    kf, logff = _hgrn_gates(ffz, lbf)
    kb, logfb = _hgrn_gates(fbz, lbb)
    o_f = _gla_scan(q, heads(kf), v, heads(logff))
    flip = lambda t: jnp.flip(t, axis=1)
    o_b = flip(_gla_scan(flip(q), flip(heads(kb)), flip(v), flip(heads(logfb))))
    o = o_f + o_b
    o = o * lax.rsqrt(jnp.mean(o * o, axis=-1, keepdims=True) + NORM_EPS)
    o = o.reshape(bsz, seq, HG_DIM) * hg_norm_w.astype(jnp.float32)
    o = (o * jax.nn.silu(ogz.astype(jnp.float32))).astype(h.dtype)
    pb = o @ w_pb

    merged = jax.nn.sigmoid(ga) * pa + jax.nn.sigmoid(gb) * pb
    return merged @ w_out


def _peer_block(xb, wq, sub_keys, u, v):
    t = xb.shape[0]
    q = (xb @ wq).astype(jnp.float32).reshape(t, PEER_HEADS, 2, PEER_HALF)
    s = jnp.einsum('thpd,hpkd->thpk', q, sub_keys.astype(jnp.float32))
    sv, si = lax.top_k(s, PEER_TOPK)
    cand_s = (sv[:, :, 0, :, None] + sv[:, :, 1, None, :]).reshape(t, PEER_HEADS, PEER_TOPK * PEER_TOPK)
    cand_i = (si[:, :, 0, :, None] * N_KEYS + si[:, :, 1, None, :]).reshape(t, PEER_HEADS, PEER_TOPK * PEER_TOPK)
    top_s, pos = lax.top_k(cand_s, PEER_TOPK)
    experts = jnp.take_along_axis(cand_i, pos, axis=-1)
    g = jax.nn.softmax(top_s, axis=-1)
    ue = jnp.take(u, experts, axis=0)
    act = jnp.einsum('td,thkd->thk', xb, ue).astype(jnp.float32)
    a = (g * jax.nn.gelu(act)).astype(xb.dtype)
    ve = jnp.take(v, experts, axis=0)
    return jnp.einsum('thk,thkd->td', a, ve)


def _peer(h, wq, sub_keys, u, v):
    bsz, seq, d = h.shape
    blocks = h.reshape(-1, PEER_BLOCK, d)
    out = lax.map(lambda xb: _peer_block(xb, wq, sub_keys, u, v), blocks)
    return out.reshape(bsz, seq, d)


def _trunk(x, norm1_w, w_in, pool_w, pool_scale, lb_fwd, lb_bwd, hg_norm_w, w_pa, w_pb, w_out,
           norm2_w, peer_wq, peer_keys, peer_u, peer_v, final_norm_w):
    for l in range(DEPTH):
        x = x + _mixer(_rms_norm(x, norm1_w[l]), l, w_in[l], pool_w[l], pool_scale[l], lb_fwd, lb_bwd,
                       hg_norm_w[l], w_pa[l], w_pb[l], w_out[l])
        x = x + _peer(_rms_norm(x, norm2_w[l]), peer_wq[l], peer_keys[l], peer_u[l], peer_v[l])
    return _rms_norm(x, final_norm_w)


def setup_inputs(seed: int = 0) -> dict:
    key = jax.random.key(seed)
    ks = jax.random.split(key, 20)
    f32 = jnp.float32
    n = lambda k, shape, s: jax.random.normal(k, shape, f32) * s
    in_cols = POOL_WIDTH + 5 * HG_DIM + 2 * D_MODEL
    return {
        "x_prompt": n(ks[0], (BATCH, SEQ, D_MODEL), 1.0),
        "x_sample": n(ks[1], (DEC_BATCH, DEC_SEQ, D_MODEL), 1.0),
        "norm1_w": 1.0 + n(ks[2], (DEPTH, D_MODEL), 0.02),
        "w_in": n(ks[3], (DEPTH, D_MODEL, in_cols), D_MODEL ** -0.5),
        "pool_w": n(ks[4], (DEPTH, POOL_GROUPS, POOL_GROUP_WIDTH, POOL_GROUP_WIDTH), POOL_GROUP_WIDTH ** -0.5),
        "pool_scale": 1.0 + n(ks[5], (DEPTH, POOL_WIDTH), 0.02),
        "lb_fwd": n(ks[6], (DEPTH + 1, HG_DIM), 0.1),
        "lb_bwd": n(ks[7], (DEPTH + 1, HG_DIM), 0.1),
        "hg_norm_w": 1.0 + n(ks[8], (DEPTH, HG_DIM), 0.02),
        "w_pa": n(ks[9], (DEPTH, POOL_WIDTH, D_MODEL), POOL_WIDTH ** -0.5),
        "w_pb": n(ks[10], (DEPTH, HG_DIM, D_MODEL), HG_DIM ** -0.5),
        "w_out": n(ks[11], (DEPTH, D_MODEL, D_MODEL), D_MODEL ** -0.5),
        "norm2_w": 1.0 + n(ks[12], (DEPTH, D_MODEL), 0.02),
        "peer_wq": n(ks[13], (DEPTH, D_MODEL, PEER_HEADS * PEER_QUERY_DIM), D_MODEL ** -0.5),
        "peer_keys": n(ks[14], (DEPTH, PEER_HEADS, 2, N_KEYS, PEER_HALF), PEER_HALF ** -0.5),
        "peer_u": n(ks[15], (DEPTH, N_EXPERTS, D_MODEL), D_MODEL ** -0.5),
        "peer_v": n(ks[16], (DEPTH, N_EXPERTS, D_MODEL), PEER_HEADS ** -0.5),
        "final_norm_w": 1.0 + n(ks[17], (D_MODEL,), 0.02),
    }


def reference(x_prompt, x_sample, norm1_w, w_in, pool_w, pool_scale, lb_fwd, lb_bwd, hg_norm_w, w_pa, w_pb,
              w_out, norm2_w, peer_wq, peer_keys, peer_u, peer_v, final_norm_w):
    y_prompt = _trunk(x_prompt, norm1_w, w_in, pool_w, pool_scale, lb_fwd, lb_bwd, hg_norm_w, w_pa, w_pb,
                      w_out, norm2_w, peer_wq, peer_keys, peer_u, peer_v, final_norm_w)
    y_sample = _trunk(x_sample, norm1_w, w_in, pool_w, pool_scale, lb_fwd, lb_bwd, hg_norm_w, w_pa, w_pb,
                      w_out, norm2_w, peer_wq, peer_keys, peer_u, peer_v, final_norm_w)
    return (y_prompt, y_sample)
```

```python
import functools

import jax
import jax.numpy as jnp
from jax import lax
from jax.experimental import pallas as pl
from jax.experimental.pallas import tpu as pltpu

F32 = jnp.float32
BF16 = jnp.bfloat16

D_MODEL = 1024
POOL_WINDOWS = (2, 4, 8, 16)
POOL_GROUP_WIDTH = 128
POOL_WIDTH = 512
POOL_HALO = 8
HG_HEADS = 8
HG_HEAD_DIM = 128
PEER_HEADS = 8
PEER_HALF = 128
N_KEYS = 128
N_EXPERTS = N_KEYS * N_KEYS
PEER_TOPK = 16
NORM_EPS = 1e-6
NOT_RANKED = 100.0
NEG_INF = float("-inf")

VMEM_LIMIT_BYTES = 56 * 1024 * 1024


def _rms(x, w):
    return x * lax.rsqrt(jnp.mean(x * x, axis=-1, keepdims=True) + NORM_EPS) * w


def _dot(a, b):
    return jnp.dot(a, b, preferred_element_type=F32)


def _dot_nt(a, b):
    return lax.dot_general(a, b, (((1,), (1,)), ((), ())), preferred_element_type=F32)


def _dot_tn(a, b):
    return lax.dot_general(a, b, (((0,), (0,)), ((), ())), preferred_element_type=F32)


def _inproj_kernel(x_ref, n1_ref, w_ref, lbf_ref, lbb_ref,
                   a_ref, q_ref, kf_ref, gf_ref, kb_ref, gb_ref, v_ref, og_ref, sga_ref, sgb_ref):
    h = _rms(x_ref[...], n1_ref[...]).astype(BF16)

    def proj(block):
        c0 = POOL_WIDTH + (block - 1) * D_MODEL if block > 0 else 0
        width = D_MODEL if block > 0 else POOL_WIDTH
        return _dot(h, w_ref[:, c0:c0 + width])

    a_ref[...] = proj(0)
    qz = proj(1)
    q_ref[...] = (qz * jax.nn.sigmoid(qz)).astype(BF16)
    for block, lb_ref, k_ref, g_ref in ((2, lbf_ref, kf_ref, gf_ref), (3, lbb_ref, kb_ref, gb_ref)):
        fz = proj(block)
        lb = lb_ref[...]
        g_ref[...] = jnp.log(lb + (1.0 - lb) * jax.nn.sigmoid(fz))
        k_ref[...] = ((1.0 - lb) * jax.nn.sigmoid(-fz)).astype(BF16)
    v_ref[...] = proj(4).astype(BF16)
    ogz = proj(5)
    og_ref[...] = (ogz * jax.nn.sigmoid(ogz)).astype(BF16)
    sga_ref[...] = jax.nn.sigmoid(proj(6)).astype(BF16)
    sgb_ref[...] = jax.nn.sigmoid(proj(7)).astype(BF16)


def _inproj(x2, n1, w_in, lbf, lbb, tm):
    t = x2.shape[0]
    row = lambda w: pl.BlockSpec((tm, w), lambda i: (i, 0))
    full = lambda a: pl.BlockSpec(a.shape, lambda i: (0,) * a.ndim)
    wide = lambda dt: jax.ShapeDtypeStruct((t, D_MODEL), dt)
    out_shape = (jax.ShapeDtypeStruct((t, POOL_WIDTH), F32),
                 wide(BF16), wide(BF16), wide(F32), wide(BF16), wide(F32),
                 wide(BF16), wide(BF16), wide(BF16), wide(BF16))
    return pl.pallas_call(
        _inproj_kernel,
        grid=(t // tm,),
        in_specs=[row(D_MODEL), full(n1), full(w_in), full(lbf), full(lbb)],
        out_specs=(row(POOL_WIDTH),) + (row(D_MODEL),) * 9,
        out_shape=out_shape,
        compiler_params=pltpu.CompilerParams(
            dimension_semantics=("parallel",), vmem_limit_bytes=VMEM_LIMIT_BYTES),
        name="inproj",
    )(x2, n1, w_in, lbf, lbb)


def _split3(x):
    hi = x.astype(BF16)
    r1 = x - hi.astype(F32)
    mid = r1.astype(BF16)
    lo = (r1 - mid.astype(F32)).astype(BF16)
    return hi, mid, lo


def _gla_direction(q_ref, k_ref, g_ref, v_ref, o_ref, s_ref, reverse):
    c = q_ref.shape[0]
    t_idx = lax.broadcasted_iota(jnp.int32, (c, c), 0)
    s_idx = lax.broadcasted_iota(jnp.int32, (c, c), 1)
    causal = (s_idx >= t_idx) if reverse else (s_idx <= t_idx)
    tri = jnp.where(causal, 1.0, 0.0).astype(BF16)
    b_all = sum(_dot(tri, part) for part in _split3(g_ref[...]))
    mid = c // 2
    end = 0 if reverse else c - 1
    for h in range(HG_HEADS):
        sl = slice(h * HG_HEAD_DIM, (h + 1) * HG_HEAD_DIM)
        b = b_all[:, sl]
        q = q_ref[:, sl].astype(F32)
        k = k_ref[:, sl].astype(F32)
        v = v_ref[:, sl]
        b_mid = b[mid:mid + 1, :]
        b_end = b[end:end + 1, :]
        qt = (q * jnp.exp(b - b_mid)).astype(BF16)
        kt = (k * jnp.exp(b_mid - b)).astype(BF16)
        att = jnp.where(causal, _dot_nt(qt, kt), 0.0).astype(BF16)
        st = s_ref[h]
        qe = (q * jnp.exp(b)).astype(BF16)
        o = _dot(att, v) + _dot_nt(qe, st.astype(BF16))
        o_ref[:, sl] = o.astype(o_ref.dtype)
        ks = (k * jnp.exp(b_end - b)).astype(BF16)
        s_ref[h] = st * jnp.exp(b_end) + _dot_tn(v, ks)


def _gla_kernel(qf_ref, kf_ref, gf_ref, vf_ref, qb_ref, kb_ref, gb_ref, vb_ref,
                of_ref, ob_ref, sf_ref, sb_ref):
    @pl.when(pl.program_id(1) == 0)
    def _():
        sf_ref[...] = jnp.zeros_like(sf_ref)
        sb_ref[...] = jnp.zeros_like(sb_ref)

    _gla_direction(qf_ref, kf_ref, gf_ref, vf_ref, of_ref, sf_ref, reverse=False)
    _gla_direction(qb_ref, kb_ref, gb_ref, vb_ref, ob_ref, sb_ref, reverse=True)


def _gla(q, kf, gf, kb, gb, v, chunk):
    bsz, seq, _ = q.shape
    n = seq // chunk
    fwd = pl.BlockSpec((None, chunk, D_MODEL), lambda b, i: (b, i, 0))
    bwd = pl.BlockSpec((None, chunk, D_MODEL), lambda b, i: (b, n - 1 - i, 0))
    state = pltpu.VMEM((HG_HEADS, HG_HEAD_DIM, HG_HEAD_DIM), F32)
    out = jax.ShapeDtypeStruct((bsz, seq, D_MODEL), BF16)
    return pl.pallas_call(
        _gla_kernel,
        grid=(bsz, n),
        in_specs=[fwd, fwd, fwd, fwd, bwd, bwd, bwd, bwd],
        out_specs=(fwd, bwd),
        out_shape=(out, out),
        scratch_shapes=[state, state],
        compiler_params=pltpu.CompilerParams(
            dimension_semantics=("parallel", "arbitrary"), vmem_limit_bytes=VMEM_LIMIT_BYTES),
        name="gla",
    )(q, kf, gf, v, q, kb, gb, v)


def _merge_kernel(seq, x_ref, ap_ref, a_ref, an_ref, of_ref, ob_ref, og_ref, sga_ref, sgb_ref,
                  pool_w_ref, pool_scale_ref, hgw_ref, w_pa_ref, w_pb_ref, w_out_ref,
                  n2_ref, wq_ref, keys_ref,
                  x1_ref, xnt_ref, st_ref):
    tm = x_ref.shape[0]
    ext = tm + 2 * POOL_HALO
    a_ext = jnp.concatenate([ap_ref[...], a_ref[...], an_ref[...]], axis=0)
    a_hi = a_ext.astype(BF16)
    a_lo = (a_ext - a_hi.astype(F32)).astype(BF16)
    t0 = pl.program_id(1) * tm
    t_row = t0 + lax.broadcasted_iota(jnp.int32, (tm, ext), 0)
    t_col = t0 - POOL_HALO + lax.broadcasted_iota(jnp.int32, (tm, ext), 1)
    t_out = t0 + lax.broadcasted_iota(jnp.int32, (tm, POOL_GROUP_WIDTH), 0)
    a_cur = a_ref[...]
    pa_parts = []
    for g, win in enumerate(POOL_WINDOWS):
        sl = slice(g * POOL_GROUP_WIDTH, (g + 1) * POOL_GROUP_WIDTH)
        lo = jnp.maximum(t_row - win // 2, 0)
        hi = jnp.minimum(t_row + win // 2, seq)
        band = jnp.where((t_col >= lo) & (t_col < hi), 1.0, 0.0).astype(BF16)
        wsum = _dot(band, a_hi[:, sl]) + _dot(band, a_lo[:, sl])
        cnt = (jnp.minimum(t_out + win // 2, seq) - jnp.maximum(t_out - win // 2, 0)).astype(F32)
        pooled = wsum / cnt - a_cur[:, sl]
        pa_parts.append(_dot(pooled.astype(BF16), pool_w_ref[g]))
    pa = jnp.concatenate(pa_parts, axis=-1) * pool_scale_ref[...]
    pa = _dot(pa.astype(BF16), w_pa_ref[...])

    o = of_ref[...].astype(F32) + ob_ref[...].astype(F32)
    o_parts = []
    for h in range(HG_HEADS):
        oh = o[:, h * HG_HEAD_DIM:(h + 1) * HG_HEAD_DIM]
        o_parts.append(oh * lax.rsqrt(jnp.mean(oh * oh, axis=-1, keepdims=True) + NORM_EPS))
    o = jnp.concatenate(o_parts, axis=-1) * hgw_ref[...] * og_ref[...].astype(F32)
    pb = _dot(o.astype(BF16), w_pb_ref[...])

    merged = sga_ref[...].astype(F32) * pa + sgb_ref[...].astype(F32) * pb
    x1 = x_ref[...] + _dot(merged.astype(BF16), w_out_ref[...])
    x1_ref[...] = x1

    xn = _rms(x1, n2_ref[...])
    xnt_ref[...] = xn.T.astype(BF16)
    qp = _dot(xn.astype(BF16), wq_ref[...]).astype(BF16)
    for j in range(2 * PEER_HEADS):
        st_ref[j] = _dot_nt(keys_ref[j], qp[:, j * PEER_HALF:(j + 1) * PEER_HALF])


def _merge(x, a, o_f, o_b, og, sga, sgb, wts, tm):
    bsz, seq, _ = x.shape
    nt = seq // tm
    t = bsz * seq
    hb = tm // POOL_HALO
    last_halo = seq // POOL_HALO - 1
    row = lambda w: pl.BlockSpec((None, tm, w), lambda b, i: (b, i, 0))
    prev = pl.BlockSpec((None, POOL_HALO, POOL_WIDTH), lambda b, i: (b, jnp.maximum(i * hb - 1, 0), 0))
    nxt = pl.BlockSpec((None, POOL_HALO, POOL_WIDTH), lambda b, i: (b, jnp.minimum((i + 1) * hb, last_halo), 0))
    full = lambda w: pl.BlockSpec(w.shape, lambda b, i: (0,) * w.ndim)
    return pl.pallas_call(
        functools.partial(_merge_kernel, seq),
        grid=(bsz, nt),
        in_specs=[row(D_MODEL), prev, row(POOL_WIDTH), nxt] + [row(D_MODEL)] * 5 + [full(w) for w in wts],
        out_specs=(row(D_MODEL),
                   pl.BlockSpec((D_MODEL, tm), lambda b, i: (0, b * nt + i)),
                   pl.BlockSpec((2 * PEER_HEADS, N_KEYS, tm), lambda b, i: (0, 0, b * nt + i))),
        out_shape=(jax.ShapeDtypeStruct((bsz, seq, D_MODEL), F32),
                   jax.ShapeDtypeStruct((D_MODEL, t), BF16),
                   jax.ShapeDtypeStruct((2 * PEER_HEADS, N_KEYS, t), F32)),
        compiler_params=pltpu.CompilerParams(
            dimension_semantics=("parallel", "parallel"), vmem_limit_bytes=VMEM_LIMIT_BYTES),
        name="merge",
    )(x, a, a, a, o_f, o_b, og, sga, sgb, *wts)


def _extract_top(s, with_rank):
    rem = s
    rank = jnp.full(s.shape, NOT_RANKED, F32) if with_rank else None
    tops = []
    for r in range(PEER_TOPK):
        m = jnp.max(rem, axis=0, keepdims=True)
        hit = rem == m
        tops.append(m)
        if with_rank:
            rank = jnp.where(hit, float(r + 1), rank)
        rem = jnp.where(hit, NEG_INF, rem)
    return tops, rank


def _select_kernel(s_ref, rank2_ref, p2_ref, nsel_ref, p1_ref):
    s1 = s_ref[0]
    s2 = s_ref[1]
    tl = s1.shape[1]
    top1, _ = _extract_top(s1, with_rank=False)
    top2, rank2 = _extract_top(s2, with_rank=True)
    row = lax.broadcasted_iota(jnp.int32, (PEER_TOPK, tl), 0)
    b_stack = jnp.zeros((PEER_TOPK, tl), F32)
    for c in range(PEER_TOPK):
        b_stack = jnp.where(row == c, top2[c], b_stack)
    cands = [top1[r] + b_stack for r in range(PEER_TOPK)]
    rem = list(cands)
    thr = None
    for _ in range(PEER_TOPK):
        m = functools.reduce(jnp.maximum, rem)
        thr = jnp.max(m, axis=0, keepdims=True)
        rem = [jnp.where(x == thr, NEG_INF, x) for x in rem]
    best = top1[0] + top2[0]
    z = jnp.zeros((1, tl), F32)
    nsel = jnp.zeros(s1.shape, F32)
    for r in range(PEER_TOPK):
        chosen = cands[r] >= thr
        n_r = jnp.sum(jnp.where(chosen, 1.0, 0.0), axis=0, keepdims=True)
        z = z + jnp.sum(jnp.where(chosen, jnp.exp(cands[r] - best), 0.0), axis=0, keepdims=True)
        nsel = jnp.where(s1 == top1[r], n_r, nsel)
    rank2_ref[...] = rank2
    nsel_ref[...] = nsel
    p1_ref[...] = jnp.exp(s1 - top1[0])
    p2_ref[...] = jnp.exp(s2 - top2[0]) / z


def _select(s_t, tl):
    t = s_t.shape[-1]
    s4 = s_t.reshape(PEER_HEADS, 2, N_KEYS, t)
    out = jax.ShapeDtypeStruct((PEER_HEADS, N_KEYS, t), F32)
    spec = pl.BlockSpec((None, N_KEYS, tl), lambda i, h: (h, 0, i))
    return pl.pallas_call(
        _select_kernel,
        grid=(t // tl, PEER_HEADS),
        in_specs=[pl.BlockSpec((None, 2, N_KEYS, tl), lambda i, h: (h, 0, 0, i))],
        out_specs=(spec,) * 4,
        out_shape=(out,) * 4,
        compiler_params=pltpu.CompilerParams(
            dimension_semantics=("parallel", "parallel"), vmem_limit_bytes=VMEM_LIMIT_BYTES),
        name="select",
    )(s4)


def _peer_kernel(xnt_ref, u_ref, vt_ref, rank2_ref, p2_ref, nsel_ref, p1_ref, x1_ref, fnw_ref,
                 y_ref, acc_ref, act_ref, gate_ref):
    c = pl.program_id(1)
    ni = nsel_ref.shape[1]

    @pl.when(c == 0)
    def _():
        acc_ref[...] = jnp.zeros_like(acc_ref)

    act_ref[...] = _dot(u_ref[...], xnt_ref[...])
    for il in range(ni):
        rows = slice(il * N_KEYS, (il + 1) * N_KEYS)
        w = jnp.zeros((N_KEYS, act_ref.shape[1]), F32)
        for h in range(PEER_HEADS):
            picked = rank2_ref[h] <= nsel_ref[h, il:il + 1, :]
            w = w + jnp.where(picked, p2_ref[h], 0.0) * p1_ref[h, il:il + 1, :]
        gate_ref[rows, :] = (jax.nn.gelu(act_ref[rows, :]) * w).astype(BF16)
    acc_ref[...] += _dot(vt_ref[...], gate_ref[...])

    @pl.when(c == pl.num_programs(1) - 1)
    def _():
        y_ref[...] = _rms(x1_ref[...] + acc_ref[...].T, fnw_ref[...])


def _peer(xn_t, u, v_t, rank2, p2, nsel, p1, x1, fnw, tm, ec):
    t = xn_t.shape[1]
    ni = ec // N_KEYS
    tok3 = pl.BlockSpec((PEER_HEADS, N_KEYS, tm), lambda i, c: (0, 0, i))
    chunk3 = pl.BlockSpec((PEER_HEADS, ni, tm), lambda i, c: (0, c, i))
    return pl.pallas_call(
        _peer_kernel,
        grid=(t // tm, N_EXPERTS // ec),
        in_specs=[pl.BlockSpec((D_MODEL, tm), lambda i, c: (0, i)),
                  pl.BlockSpec((ec, D_MODEL), lambda i, c: (c, 0)),
                  pl.BlockSpec((D_MODEL, ec), lambda i, c: (0, c)),
                  tok3, tok3, chunk3, chunk3,
                  pl.BlockSpec((tm, D_MODEL), lambda i, c: (i, 0)),
                  pl.BlockSpec((1, D_MODEL), lambda i, c: (0, 0))],
        out_specs=pl.BlockSpec((tm, D_MODEL), lambda i, c: (i, 0)),
        out_shape=jax.ShapeDtypeStruct((t, D_MODEL), F32),
        scratch_shapes=[pltpu.VMEM((D_MODEL, tm), F32),
                        pltpu.VMEM((ec, tm), F32),
                        pltpu.VMEM((ec, tm), BF16)],
        compiler_params=pltpu.CompilerParams(
            dimension_semantics=("parallel", "arbitrary"), vmem_limit_bytes=VMEM_LIMIT_BYTES),
        name="peer",
    )(xn_t, u, v_t, rank2, p2, nsel, p1, x1, fnw)


def _tile(n, pref):
    while n % pref:
        pref //= 2
    return pref


def _trunk(x, w):
    bsz, seq, d = x.shape
    t = bsz * seq
    a, q, kf, gf, kb, gb, v, og, sga, sgb = _inproj(
        x.reshape(t, d), w["norm1"], w["w_in"], w["lbf"], w["lbb"], _tile(t, 512))
    r3 = lambda z: z.reshape(bsz, seq, z.shape[-1])
    o_f, o_b = _gla(r3(q), r3(kf), r3(gf), r3(kb), r3(gb), r3(v), _tile(seq, 64))
    x1, xn_t, s_t = _merge(x, r3(a), o_f, o_b, r3(og), r3(sga), r3(sgb), w["merge"], _tile(seq, 256))
    rank2, p2, nsel, p1 = _select(s_t, _tile(t, 512))
    y = _peer(xn_t, w["u"], w["v_t"], rank2, p2, nsel, p1, x1.reshape(t, d), w["final_norm"],
              _tile(t, 512), 1024)
    return y.reshape(bsz, seq, d)


def kernel(x_prompt, x_sample, norm1_w, w_in, pool_w, pool_scale, lb_fwd, lb_bwd, hg_norm_w, w_pa, w_pb,
           w_out, norm2_w, peer_wq, peer_keys, peer_u, peer_v, final_norm_w):
    layer = 0
    lower_bound = lambda lb: jnp.cumsum(jax.nn.softmax(lb.astype(F32), axis=0), axis=0)[layer][None, :]
    row = lambda p: p.astype(F32).reshape(1, -1)
    w = {
        "norm1": row(norm1_w[layer]),
        "w_in": w_in[layer].astype(BF16),
        "lbf": lower_bound(lb_fwd),
        "lbb": lower_bound(lb_bwd),
        "merge": (pool_w[layer].astype(BF16), row(pool_scale[layer]), row(hg_norm_w[layer]),
                  w_pa[layer].astype(BF16), w_pb[layer].astype(BF16), w_out[layer].astype(BF16),
                  row(norm2_w[layer]), peer_wq[layer].astype(BF16),
                  peer_keys[layer].astype(BF16).reshape(2 * PEER_HEADS, N_KEYS, PEER_HALF)),
        "u": peer_u[layer].astype(BF16),
        "v_t": peer_v[layer].T.astype(BF16),
        "final_norm": row(final_norm_w),
    }
    return _trunk(x_prompt, w), _trunk(x_sample, w)
```

```python
import functools

import jax
import jax.numpy as jnp
from jax import lax
from jax.experimental import pallas as pl
from jax.experimental.pallas import tpu as pltpu

F32 = jnp.float32
BF16 = jnp.bfloat16

D_MODEL = 1024
POOL_WINDOWS = (2, 4, 8, 16)
POOL_GROUP_WIDTH = 128
POOL_WIDTH = 512
POOL_HALO = 8
HG_HEADS = 8
HG_HEAD_DIM = 128
PEER_HEADS = 8
PEER_HALF = 128
N_KEYS = 128
N_EXPERTS = N_KEYS * N_KEYS
PEER_TOPK = 16
NORM_EPS = 1e-6
NOT_RANKED = 100.0
NEG_INF = float("-inf")
PEER_PIECE = 256
GELU_C0 = 0.7978845608028654
GELU_C1 = GELU_C0 * 0.044715

VMEM_LIMIT_BYTES = 56 * 1024 * 1024


def _rms(x, w):
    return x * lax.rsqrt(jnp.mean(x * x, axis=-1, keepdims=True) + NORM_EPS) * w


def _dot(a, b):
    return jnp.dot(a, b, preferred_element_type=F32)


def _dot_nt(a, b):
    return lax.dot_general(a, b, (((1,), (1,)), ((), ())), preferred_element_type=F32)


def _dot_tn(a, b):
    return lax.dot_general(a, b, (((0,), (0,)), ((), ())), preferred_element_type=F32)


def _inproj_kernel(x_ref, n1_ref, w_ref, lbf_ref, lbb_ref,
                   a_ref, q_ref, kf_ref, gf_ref, kb_ref, gb_ref, v_ref, og_ref, sga_ref, sgb_ref):
    h = _rms(x_ref[...], n1_ref[...]).astype(BF16)

    def proj(block):
        c0 = POOL_WIDTH + (block - 1) * D_MODEL if block > 0 else 0
        width = D_MODEL if block > 0 else POOL_WIDTH
        return _dot(h, w_ref[:, c0:c0 + width])

    a_ref[...] = proj(0)
    qz = proj(1)
    q_ref[...] = (qz * jax.nn.sigmoid(qz)).astype(BF16)
    for block, lb_ref, k_ref, g_ref in ((2, lbf_ref, kf_ref, gf_ref), (3, lbb_ref, kb_ref, gb_ref)):
        fz = proj(block)
        lb = lb_ref[...]
        g_ref[...] = jnp.log(lb + (1.0 - lb) * jax.nn.sigmoid(fz))
        k_ref[...] = ((1.0 - lb) * jax.nn.sigmoid(-fz)).astype(BF16)
    v_ref[...] = proj(4).astype(BF16)
    ogz = proj(5)
    og_ref[...] = (ogz * jax.nn.sigmoid(ogz)).astype(BF16)
    sga_ref[...] = jax.nn.sigmoid(proj(6)).astype(BF16)
    sgb_ref[...] = jax.nn.sigmoid(proj(7)).astype(BF16)


def _inproj(x2, n1, w_in, lbf, lbb, tm):
    t = x2.shape[0]
    row = lambda w: pl.BlockSpec((tm, w), lambda i: (i, 0))
    full = lambda a: pl.BlockSpec(a.shape, lambda i: (0,) * a.ndim)
    wide = lambda dt: jax.ShapeDtypeStruct((t, D_MODEL), dt)
    out_shape = (jax.ShapeDtypeStruct((t, POOL_WIDTH), F32),
                 wide(BF16), wide(BF16), wide(F32), wide(BF16), wide(F32),
                 wide(BF16), wide(BF16), wide(BF16), wide(BF16))
    return pl.pallas_call(
        _inproj_kernel,
        grid=(t // tm,),
        in_specs=[row(D_MODEL), full(n1), full(w_in), full(lbf), full(lbb)],
        out_specs=(row(POOL_WIDTH),) + (row(D_MODEL),) * 9,
        out_shape=out_shape,
        compiler_params=pltpu.CompilerParams(
            dimension_semantics=("parallel",), vmem_limit_bytes=VMEM_LIMIT_BYTES),
        name="inproj",
    )(x2, n1, w_in, lbf, lbb)


def _split3(x):
    hi = x.astype(BF16)
    r1 = x - hi.astype(F32)
    mid = r1.astype(BF16)
    lo = (r1 - mid.astype(F32)).astype(BF16)
    return hi, mid, lo


def _gla_direction(q_ref, k_ref, g_ref, v_ref, o_ref, s_ref, reverse):
    c = q_ref.shape[0]
    hc = c // 2

    def travel_order(n):
        t_idx = lax.broadcasted_iota(jnp.int32, (n, n), 0)
        s_idx = lax.broadcasted_iota(jnp.int32, (n, n), 1)
        return (s_idx >= t_idx) if reverse else (s_idx <= t_idx)

    causal = travel_order(hc)
    tri = jnp.where(travel_order(c), 1.0, 0.0).astype(BF16)
    b_all = sum(_dot(tri, part) for part in _split3(g_ref[...]))
    early, late = (slice(hc, c), slice(0, hc)) if reverse else (slice(0, hc), slice(hc, c))
    edge, end = (hc, 0) if reverse else (hc - 1, c - 1)
    for h in range(HG_HEADS):
        sl = slice(h * HG_HEAD_DIM, (h + 1) * HG_HEAD_DIM)
        b = b_all[:, sl]
        q = q_ref[:, sl].astype(F32)
        k = k_ref[:, sl].astype(F32)
        v = v_ref[:, sl]

        def scores(rows, cols, ref_row, masked):
            ref = b[ref_row:ref_row + 1, :]
            qt = (q[rows] * jnp.exp(b[rows] - ref)).astype(BF16)
            kt = (k[cols] * jnp.exp(ref - b[cols])).astype(BF16)
            att = _dot_nt(qt, kt)
            return (jnp.where(causal, att, 0.0) if masked else att).astype(BF16)

        att_ee = scores(early, early, early.start + hc // 2, True)
        att_ll = scores(late, late, late.start + hc // 2, True)
        att_le = scores(late, early, edge, False)
        st = s_ref[h]
        carried = _dot_nt((q * jnp.exp(b)).astype(BF16), st.astype(BF16))
        o_ref[early, sl] = (_dot(att_ee, v[early]) + carried[early]).astype(o_ref.dtype)
        o_ref[late, sl] = (_dot(att_le, v[early]) + _dot(att_ll, v[late]) + carried[late]).astype(o_ref.dtype)
        b_end = b[end:end + 1, :]
        ks = (k * jnp.exp(b_end - b)).astype(BF16)
        s_ref[h] = st * jnp.exp(b_end) + _dot_tn(v, ks)


def _gla_kernel(qf_ref, kf_ref, gf_ref, vf_ref, qb_ref, kb_ref, gb_ref, vb_ref,
                of_ref, ob_ref, sf_ref, sb_ref):
    @pl.when(pl.program_id(1) == 0)
    def _():
        sf_ref[...] = jnp.zeros_like(sf_ref)
        sb_ref[...] = jnp.zeros_like(sb_ref)

    _gla_direction(qf_ref, kf_ref, gf_ref, vf_ref, of_ref, sf_ref, reverse=False)
    _gla_direction(qb_ref, kb_ref, gb_ref, vb_ref, ob_ref, sb_ref, reverse=True)


def _gla(q, kf, gf, kb, gb, v, chunk):
    bsz, seq, _ = q.shape
    n = seq // chunk
    fwd = pl.BlockSpec((None, chunk, D_MODEL), lambda b, i: (b, i, 0))
    bwd = pl.BlockSpec((None, chunk, D_MODEL), lambda b, i: (b, n - 1 - i, 0))
    state = pltpu.VMEM((HG_HEADS, HG_HEAD_DIM, HG_HEAD_DIM), F32)
    out = jax.ShapeDtypeStruct((bsz, seq, D_MODEL), BF16)
    return pl.pallas_call(
        _gla_kernel,
        grid=(bsz, n),
        in_specs=[fwd, fwd, fwd, fwd, bwd, bwd, bwd, bwd],
        out_specs=(fwd, bwd),
        out_shape=(out, out),
        scratch_shapes=[state, state],
        compiler_params=pltpu.CompilerParams(
            dimension_semantics=("parallel", "arbitrary"), vmem_limit_bytes=VMEM_LIMIT_BYTES),
        name="gla",
    )(q, kf, gf, v, q, kb, gb, v)


def _merge_kernel(seq, x_ref, ap_ref, a_ref, an_ref, of_ref, ob_ref, og_ref, sga_ref, sgb_ref,
                  pool_w_ref, pool_scale_ref, hgw_ref, w_pa_ref, w_pb_ref, w_out_ref,
                  n2_ref, wq_ref, keys_ref,
                  x1_ref, xnt_ref, st_ref):
    tm = x_ref.shape[0]
    ext = tm + 2 * POOL_HALO
    a_ext = jnp.concatenate([ap_ref[...], a_ref[...], an_ref[...]], axis=0)
    a_hi = a_ext.astype(BF16)
    a_lo = (a_ext - a_hi.astype(F32)).astype(BF16)
    t0 = pl.program_id(1) * tm
    t_row = t0 + lax.broadcasted_iota(jnp.int32, (tm, ext), 0)
    t_col = t0 - POOL_HALO + lax.broadcasted_iota(jnp.int32, (tm, ext), 1)
    t_out = t0 + lax.broadcasted_iota(jnp.int32, (tm, POOL_GROUP_WIDTH), 0)
    a_cur = a_ref[...]
    pa_parts = []
    for g, win in enumerate(POOL_WINDOWS):
        sl = slice(g * POOL_GROUP_WIDTH, (g + 1) * POOL_GROUP_WIDTH)
        lo = jnp.maximum(t_row - win // 2, 0)
        hi = jnp.minimum(t_row + win // 2, seq)
        band = jnp.where((t_col >= lo) & (t_col < hi), 1.0, 0.0).astype(BF16)
        wsum = _dot(band, a_hi[:, sl]) + _dot(band, a_lo[:, sl])
        cnt = (jnp.minimum(t_out + win // 2, seq) - jnp.maximum(t_out - win // 2, 0)).astype(F32)
        pooled = wsum / cnt - a_cur[:, sl]
        pa_parts.append(_dot(pooled.astype(BF16), pool_w_ref[g]))
    pa = jnp.concatenate(pa_parts, axis=-1) * pool_scale_ref[...]
    pa = _dot(pa.astype(BF16), w_pa_ref[...])

    o = of_ref[...].astype(F32) + ob_ref[...].astype(F32)
    o_parts = []
    for h in range(HG_HEADS):
        oh = o[:, h * HG_HEAD_DIM:(h + 1) * HG_HEAD_DIM]
        o_parts.append(oh * lax.rsqrt(jnp.mean(oh * oh, axis=-1, keepdims=True) + NORM_EPS))
    o = jnp.concatenate(o_parts, axis=-1) * hgw_ref[...] * og_ref[...].astype(F32)
    pb = _dot(o.astype(BF16), w_pb_ref[...])

    merged = sga_ref[...].astype(F32) * pa + sgb_ref[...].astype(F32) * pb
    x1 = x_ref[...] + _dot(merged.astype(BF16), w_out_ref[...])
    x1_ref[...] = x1

    xn = _rms(x1, n2_ref[...])
    xnt_ref[...] = xn.T.astype(BF16)
    qp = _dot(xn.astype(BF16), wq_ref[...]).astype(BF16)
    for j in range(2 * PEER_HEADS):
        st_ref[j] = _dot_nt(keys_ref[j], qp[:, j * PEER_HALF:(j + 1) * PEER_HALF])


def _merge(x, a, o_f, o_b, og, sga, sgb, wts, tm):
    bsz, seq, _ = x.shape
    nt = seq // tm
    t = bsz * seq
    hb = tm // POOL_HALO
    last_halo = seq // POOL_HALO - 1
    row = lambda w: pl.BlockSpec((None, tm, w), lambda b, i: (b, i, 0))
    prev = pl.BlockSpec((None, POOL_HALO, POOL_WIDTH), lambda b, i: (b, jnp.maximum(i * hb - 1, 0), 0))
    nxt = pl.BlockSpec((None, POOL_HALO, POOL_WIDTH), lambda b, i: (b, jnp.minimum((i + 1) * hb, last_halo), 0))
    full = lambda w: pl.BlockSpec(w.shape, lambda b, i: (0,) * w.ndim)
    return pl.pallas_call(
        functools.partial(_merge_kernel, seq),
        grid=(bsz, nt),
        in_specs=[row(D_MODEL), prev, row(POOL_WIDTH), nxt] + [row(D_MODEL)] * 5 + [full(w) for w in wts],
        out_specs=(row(D_MODEL),
                   pl.BlockSpec((D_MODEL, tm), lambda b, i: (0, b * nt + i)),
                   pl.BlockSpec((2 * PEER_HEADS, N_KEYS, tm), lambda b, i: (0, 0, b * nt + i))),
        out_shape=(jax.ShapeDtypeStruct((bsz, seq, D_MODEL), F32),
                   jax.ShapeDtypeStruct((D_MODEL, t), BF16),
                   jax.ShapeDtypeStruct((2 * PEER_HEADS, N_KEYS, t), F32)),
        compiler_params=pltpu.CompilerParams(
            dimension_semantics=("parallel", "parallel"), vmem_limit_bytes=VMEM_LIMIT_BYTES),
        name="merge",
    )(x, a, a, a, o_f, o_b, og, sga, sgb, *wts)


def _extract_top(s, with_rank):
    rem = s
    rank = jnp.full(s.shape, NOT_RANKED, F32) if with_rank else None
    tops = []
    for r in range(PEER_TOPK):
        m = jnp.max(rem, axis=0, keepdims=True)
        hit = rem == m
        tops.append(m)
        if with_rank:
            rank = jnp.where(hit, float(r + 1), rank)
        rem = jnp.where(hit, NEG_INF, rem)
    return tops, rank


def _select_kernel(s_ref, rank2_ref, p2_ref, nsel_ref, p1_ref):
    s1 = s_ref[0]
    s2 = s_ref[1]
    tl = s1.shape[1]
    half = PEER_TOPK // 2
    top1, _ = _extract_top(s1, with_rank=False)
    top2, rank2 = _extract_top(s2, with_rank=True)
    row = lax.broadcasted_iota(jnp.int32, (PEER_TOPK, tl), 0)
    a_stack = jnp.zeros((PEER_TOPK, tl), F32)
    b_stack = jnp.zeros((PEER_TOPK, tl), F32)
    for c in range(PEER_TOPK):
        a_stack = jnp.where(row == c, top1[c], a_stack)
        b_stack = jnp.where(row == c, top2[c], b_stack)
    cands = [top1[0] + b_stack]
    cands += [top1[r] + b_stack[:half] for r in range(1, half)]
    cands += [a_stack[half:] + top2[0]]
    rem = list(cands)
    thr = None
    for _ in range(PEER_TOPK):
        thr = functools.reduce(jnp.maximum, [jnp.max(x, axis=0, keepdims=True) for x in rem])
        rem = [jnp.where(x == thr, NEG_INF, x) for x in rem]
    best = top1[0] + top2[0]
    chosen = [x >= thr for x in cands]
    z = sum(jnp.sum(jnp.where(ch, jnp.exp(x - best), 0.0), axis=0, keepdims=True)
            for ch, x in zip(chosen, cands))
    count = [jnp.where(ch, 1.0, 0.0) for ch in chosen]
    nsel = jnp.zeros(s1.shape, F32)
    for r in range(half):
        nsel = jnp.where(s1 == top1[r], jnp.sum(count[r], axis=0, keepdims=True), nsel)
    for r in range(half, PEER_TOPK):
        nsel = jnp.where(s1 == top1[r], count[half][r - half:r - half + 1], nsel)
    rank2_ref[...] = rank2.astype(rank2_ref.dtype)
    nsel_ref[...] = nsel
    p1_ref[...] = jnp.exp(s1 - top1[0])
    p2_ref[...] = (jnp.exp(s2 - top2[0]) * (0.5 / z)).astype(p2_ref.dtype)


def _select(s_t, tl):
    t = s_t.shape[-1]
    s4 = s_t.reshape(PEER_HEADS, 2, N_KEYS, t)
    out = lambda dt: jax.ShapeDtypeStruct((PEER_HEADS, N_KEYS, t), dt)
    spec = pl.BlockSpec((None, N_KEYS, tl), lambda i, h: (h, 0, i))
    return pl.pallas_call(
        _select_kernel,
        grid=(t // tl, PEER_HEADS),
        in_specs=[pl.BlockSpec((None, 2, N_KEYS, tl), lambda i, h: (h, 0, 0, i))],
        out_specs=(spec,) * 4,
        out_shape=(out(BF16), out(BF16), out(F32), out(F32)),
        compiler_params=pltpu.CompilerParams(
            dimension_semantics=("parallel", "parallel"), vmem_limit_bytes=VMEM_LIMIT_BYTES),
        name="select",
    )(s4)


def _peer_gates(act, rank2_ref, p2_ref, nsel_ref, p1_ref, gate_ref, slot, cols):
    tp = act.shape[1]
    for il in range(act.shape[0] // N_KEYS):
        x = act[il * N_KEYS:(il + 1) * N_KEYS, :].astype(BF16)
        w = None
        for h in range(PEER_HEADS):
            ns = jnp.broadcast_to(nsel_ref[h, il:il + 1, cols], (N_KEYS, tp)).astype(BF16)
            p1 = jnp.broadcast_to(p1_ref[h, il:il + 1, cols], (N_KEYS, tp)).astype(BF16)
            term = jnp.where(rank2_ref[h, :, cols] <= ns, p2_ref[h, :, cols], 0) * p1
            w = term if w is None else w + term
        inner = x * (GELU_C0 + GELU_C1 * (x * x))
        gate_ref[slot, il * N_KEYS:(il + 1) * N_KEYS, cols] = (x + x * jnp.tanh(inner)) * w


def _peer_kernel(xnt_ref, u_ref, vt_ref, rank2_ref, p2_ref, nsel_ref, p1_ref, x1_ref, fnw_ref,
                 y_ref, acc_ref, gate_ref):
    c = pl.program_id(1)
    last = pl.num_programs(1) - 1
    slot = lax.rem(c, 2)
    tm = xnt_ref.shape[1]
    pieces = [slice(p * PEER_PIECE, (p + 1) * PEER_PIECE) for p in range(tm // PEER_PIECE)]

    def pre_activations():
        return [_dot(u_ref[...], xnt_ref[:, cols]) for cols in pieces]

    def v_product(prev):
        return [_dot(vt_ref[...], gate_ref[prev, :, cols]) for cols in pieces]

    def build(acts, slot):
        for act, cols in zip(acts, pieces):
            _peer_gates(act, rank2_ref, p2_ref, nsel_ref, p1_ref, gate_ref, slot, cols)

    @pl.when(c == 0)
    def _():
        acc_ref[...] = jnp.zeros_like(acc_ref)
        build(pre_activations(), 0)

    @pl.when((c > 0) & (c < last))
    def _():
        acts = pre_activations()
        for upd, cols in zip(v_product(1 - slot), pieces):
            acc_ref[:, cols] += upd
        build(acts, slot)

    @pl.when(c == last)
    def _():
        out = acc_ref[...] + jnp.concatenate(v_product(1 - slot), axis=1)
        y_ref[...] = _rms(x1_ref[...] + out.T, fnw_ref[...])


def _peer(xn_t, u, v_t, rank2, p2, nsel, p1, x1, fnw, tm, ec):
    t = xn_t.shape[1]
    ni = ec // N_KEYS
    nch = N_EXPERTS // ec
    chunk = lambda c, lag: jnp.clip(c - lag, 0, nch - 1)
    tok3 = pl.BlockSpec((PEER_HEADS, N_KEYS, tm), lambda i, c: (0, 0, i))
    chunk3 = pl.BlockSpec((PEER_HEADS, ni, tm), lambda i, c: (0, chunk(c, 0), i))
    return pl.pallas_call(
        _peer_kernel,
        grid=(t // tm, nch + 1),
        in_specs=[pl.BlockSpec((D_MODEL, tm), lambda i, c: (0, i)),
                  pl.BlockSpec((ec, D_MODEL), lambda i, c: (chunk(c, 0), 0)),
                  pl.BlockSpec((D_MODEL, ec), lambda i, c: (0, chunk(c, 1))),
                  tok3, tok3, chunk3, chunk3,
                  pl.BlockSpec((tm, D_MODEL), lambda i, c: (i, 0)),
                  pl.BlockSpec((1, D_MODEL), lambda i, c: (0, 0))],
        out_specs=pl.BlockSpec((tm, D_MODEL), lambda i, c: (i, 0)),
        out_shape=jax.ShapeDtypeStruct((t, D_MODEL), F32),
        scratch_shapes=[pltpu.VMEM((D_MODEL, tm), F32), pltpu.VMEM((2, ec, tm), BF16)],
        compiler_params=pltpu.CompilerParams(
            dimension_semantics=("parallel", "arbitrary"), vmem_limit_bytes=VMEM_LIMIT_BYTES),
        name="peer",
    )(xn_t, u, v_t, rank2, p2, nsel, p1, x1, fnw)


def _tile(n, pref):
    while n % pref:
        pref //= 2
    return pref


def _trunk(x, w):
    bsz, seq, d = x.shape
    t = bsz * seq
    a, q, kf, gf, kb, gb, v, og, sga, sgb = _inproj(
        x.reshape(t, d), w["norm1"], w["w_in"], w["lbf"], w["lbb"], _tile(t, 512))
    r3 = lambda z: z.reshape(bsz, seq, z.shape[-1])
    o_f, o_b = _gla(r3(q), r3(kf), r3(gf), r3(kb), r3(gb), r3(v), _tile(seq, 128))
    x1, xn_t, s_t = _merge(x, r3(a), o_f, o_b, r3(og), r3(sga), r3(sgb), w["merge"], _tile(seq, 256))
    rank2, p2, nsel, p1 = _select(s_t, _tile(t, 512))
    y = _peer(xn_t, w["u"], w["v_t"], rank2, p2, nsel, p1, x1.reshape(t, d), w["final_norm"],
              _tile(t, 512), 1024)
    return y.reshape(bsz, seq, d)


def kernel(x_prompt, x_sample, norm1_w, w_in, pool_w, pool_scale, lb_fwd, lb_bwd, hg_norm_w, w_pa, w_pb,
           w_out, norm2_w, peer_wq, peer_keys, peer_u, peer_v, final_norm_w):
    layer = 0
    lower_bound = lambda lb: jnp.cumsum(jax.nn.softmax(lb.astype(F32), axis=0), axis=0)[layer][None, :]
    row = lambda p: p.astype(F32).reshape(1, -1)
    w = {
        "norm1": row(norm1_w[layer]),
        "w_in": w_in[layer].astype(BF16),
        "lbf": lower_bound(lb_fwd),
        "lbb": lower_bound(lb_bwd),
        "merge": (pool_w[layer].astype(BF16), row(pool_scale[layer]), row(hg_norm_w[layer]),
                  w_pa[layer].astype(BF16), w_pb[layer].astype(BF16), w_out[layer].astype(BF16),
                  row(norm2_w[layer]), peer_wq[layer].astype(BF16),
                  peer_keys[layer].astype(BF16).reshape(2 * PEER_HEADS, N_KEYS, PEER_HALF)),
        "u": peer_u[layer].astype(BF16),
        "v_t": peer_v[layer].T.astype(BF16),
        "final_norm": row(final_norm_w),
    }
    return _trunk(x_prompt, w), _trunk(x_sample, w)
```

```python
import functools

import jax
import jax.numpy as jnp
from jax import lax
from jax.experimental import pallas as pl
from jax.experimental.pallas import tpu as pltpu

F32 = jnp.float32
BF16 = jnp.bfloat16

D_MODEL = 1024
POOL_WINDOWS = (2, 4, 8, 16)
POOL_GROUP_WIDTH = 128
POOL_WIDTH = 512
SUBLANES = 8
POOL_HALO = SUBLANES
HG_HEADS = 8
HG_HEAD_DIM = 128
PEER_HEADS = 8
PEER_HALF = 128
N_KEYS = 128
N_EXPERTS = N_KEYS * N_KEYS
PEER_TOPK = 16
NORM_EPS = 1e-6
NOT_RANKED = 100.0
NEG_INF = float("-inf")
PEER_PIECE = 256
PEER_CHUNK = 2048
GELU_C0 = 0.7978845608028654
GELU_C1 = GELU_C0 * 0.044715

VMEM_LIMIT_BYTES = 56 * 1024 * 1024


def _rms(x, w):
    return x * lax.rsqrt(jnp.mean(x * x, axis=-1, keepdims=True) + NORM_EPS) * w


def _dot(a, b):
    return jnp.dot(a, b, preferred_element_type=F32)


def _dot_nt(a, b):
    return lax.dot_general(a, b, (((1,), (1,)), ((), ())), preferred_element_type=F32)


def _dot_tn(a, b):
    return lax.dot_general(a, b, (((0,), (0,)), ((), ())), preferred_element_type=F32)


def _inproj_kernel(x_ref, n1_ref, w_ref, lbf_ref, lbb_ref,
                   a_ref, q_ref, kf_ref, gf_ref, kb_ref, gb_ref, v_ref, og_ref, sga_ref, sgb_ref):
    h = _rms(x_ref[...], n1_ref[...]).astype(BF16)

    def proj(block):
        c0 = POOL_WIDTH + (block - 1) * D_MODEL if block > 0 else 0
        width = D_MODEL if block > 0 else POOL_WIDTH
        return _dot(h, w_ref[:, c0:c0 + width])

    a_ref[...] = proj(0)
    qz = proj(1)
    q_ref[...] = (qz * jax.nn.sigmoid(qz)).astype(BF16)
    for block, lb_ref, k_ref, g_ref in ((2, lbf_ref, kf_ref, gf_ref), (3, lbb_ref, kb_ref, gb_ref)):
        fz = proj(block)
        lb = lb_ref[...]
        g_ref[...] = jnp.log(lb + (1.0 - lb) * jax.nn.sigmoid(fz))
        k_ref[...] = ((1.0 - lb) * jax.nn.sigmoid(-fz)).astype(BF16)
    v_ref[...] = proj(4).astype(BF16)
    ogz = proj(5)
    og_ref[...] = (ogz * jax.nn.sigmoid(ogz)).astype(BF16)
    sga_ref[...] = jax.nn.sigmoid(proj(6)).astype(BF16)
    sgb_ref[...] = jax.nn.sigmoid(proj(7)).astype(BF16)


def _inproj(x2, n1, w_in, lbf, lbb, tm):
    t = x2.shape[0]
    row = lambda w: pl.BlockSpec((tm, w), lambda i: (i, 0))
    full = lambda a: pl.BlockSpec(a.shape, lambda i: (0,) * a.ndim)
    wide = lambda dt: jax.ShapeDtypeStruct((t, D_MODEL), dt)
    out_shape = (jax.ShapeDtypeStruct((t, POOL_WIDTH), F32),
                 wide(BF16), wide(BF16), wide(F32), wide(BF16), wide(F32),
                 wide(BF16), wide(BF16), wide(BF16), wide(BF16))
    return pl.pallas_call(
        _inproj_kernel,
        grid=(t // tm,),
        in_specs=[row(D_MODEL), full(n1), full(w_in), full(lbf), full(lbb)],
        out_specs=(row(POOL_WIDTH),) + (row(D_MODEL),) * 9,
        out_shape=out_shape,
        compiler_params=pltpu.CompilerParams(
            dimension_semantics=("parallel",), vmem_limit_bytes=VMEM_LIMIT_BYTES),
        name="inproj",
    )(x2, n1, w_in, lbf, lbb)


def _split3(x):
    hi = x.astype(BF16)
    r1 = x - hi.astype(F32)
    mid = r1.astype(BF16)
    lo = (r1 - mid.astype(F32)).astype(BF16)
    return hi, mid, lo


def _gla_direction(q_ref, k_ref, g_ref, v_ref, o_ref, s_ref, reverse):
    c = q_ref.shape[0]
    hc = c // 2

    def travel_order(n):
        t_idx = lax.broadcasted_iota(jnp.int32, (n, n), 0)
        s_idx = lax.broadcasted_iota(jnp.int32, (n, n), 1)
        return (s_idx >= t_idx) if reverse else (s_idx <= t_idx)

    causal = travel_order(hc)
    tri = jnp.where(travel_order(c), 1.0, 0.0).astype(BF16)
    b_all = sum(_dot(tri, part) for part in _split3(g_ref[...]))
    early, late = (slice(hc, c), slice(0, hc)) if reverse else (slice(0, hc), slice(hc, c))
    edge, end = (hc, 0) if reverse else (hc - 1, c - 1)
    for h in range(HG_HEADS):
        sl = slice(h * HG_HEAD_DIM, (h + 1) * HG_HEAD_DIM)
        b = b_all[:, sl]
        q = q_ref[:, sl].astype(F32)
        k = k_ref[:, sl].astype(F32)
        v = v_ref[:, sl]

        def scores(rows, cols, ref_row, masked):
            ref = b[ref_row:ref_row + 1, :]
            qt = (q[rows] * jnp.exp(b[rows] - ref)).astype(BF16)
            kt = (k[cols] * jnp.exp(ref - b[cols])).astype(BF16)
            att = _dot_nt(qt, kt)
            return (jnp.where(causal, att, 0.0) if masked else att).astype(BF16)

        att_ee = scores(early, early, early.start + hc // 2, True)
        att_ll = scores(late, late, late.start + hc // 2, True)
        att_le = scores(late, early, edge, False)
        st = s_ref[h]
        carried = _dot_nt((q * jnp.exp(b)).astype(BF16), st.astype(BF16))
        o_ref[early, sl] = (_dot(att_ee, v[early]) + carried[early]).astype(o_ref.dtype)
        o_ref[late, sl] = (_dot(att_le, v[early]) + _dot(att_ll, v[late]) + carried[late]).astype(o_ref.dtype)
        b_end = b[end:end + 1, :]
        ks = (k * jnp.exp(b_end - b)).astype(BF16)
        s_ref[h] = st * jnp.exp(b_end) + _dot_tn(v, ks)


def _gla_kernel(qf_ref, kf_ref, gf_ref, vf_ref, qb_ref, kb_ref, gb_ref, vb_ref,
                of_ref, ob_ref, sf_ref, sb_ref):
    @pl.when(pl.program_id(1) == 0)
    def _():
        sf_ref[...] = jnp.zeros_like(sf_ref)
        sb_ref[...] = jnp.zeros_like(sb_ref)

    _gla_direction(qf_ref, kf_ref, gf_ref, vf_ref, of_ref, sf_ref, reverse=False)
    _gla_direction(qb_ref, kb_ref, gb_ref, vb_ref, ob_ref, sb_ref, reverse=True)


def _gla(q, kf, gf, kb, gb, v, chunk):
    bsz, seq, _ = q.shape
    n = seq // chunk
    fwd = pl.BlockSpec((None, chunk, D_MODEL), lambda b, i: (b, i, 0))
    bwd = pl.BlockSpec((None, chunk, D_MODEL), lambda b, i: (b, n - 1 - i, 0))
    state = pltpu.VMEM((HG_HEADS, HG_HEAD_DIM, HG_HEAD_DIM), F32)
    out = jax.ShapeDtypeStruct((bsz, seq, D_MODEL), BF16)
    return pl.pallas_call(
        _gla_kernel,
        grid=(bsz, n),
        in_specs=[fwd, fwd, fwd, fwd, bwd, bwd, bwd, bwd],
        out_specs=(fwd, bwd),
        out_shape=(out, out),
        scratch_shapes=[state, state],
        compiler_params=pltpu.CompilerParams(
            dimension_semantics=("parallel", "arbitrary"), vmem_limit_bytes=VMEM_LIMIT_BYTES),
        name="gla",
    )(q, kf, gf, v, q, kb, gb, v)


def _merge_kernel(seq, x_ref, ap_ref, a_ref, an_ref, of_ref, ob_ref, og_ref, sga_ref, sgb_ref,
                  pool_w_ref, pool_scale_ref, hgw_ref, w_pa_ref, w_pb_ref, w_out_ref,
                  n2_ref, wq_ref, keys_ref,
                  x1_ref, xnt_ref, st_ref):
    tm = x_ref.shape[0]
    ext = tm + 2 * POOL_HALO
    a_ext = jnp.concatenate([ap_ref[...], a_ref[...], an_ref[...]], axis=0)
    a_hi = a_ext.astype(BF16)
    a_lo = (a_ext - a_hi.astype(F32)).astype(BF16)
    t0 = pl.program_id(1) * tm
    t_row = t0 + lax.broadcasted_iota(jnp.int32, (tm, ext), 0)
    t_col = t0 - POOL_HALO + lax.broadcasted_iota(jnp.int32, (tm, ext), 1)
    t_out = t0 + lax.broadcasted_iota(jnp.int32, (tm, POOL_GROUP_WIDTH), 0)
    a_cur = a_ref[...]
    pa_parts = []
    for g, win in enumerate(POOL_WINDOWS):
        sl = slice(g * POOL_GROUP_WIDTH, (g + 1) * POOL_GROUP_WIDTH)
        lo = jnp.maximum(t_row - win // 2, 0)
        hi = jnp.minimum(t_row + win // 2, seq)
        band = jnp.where((t_col >= lo) & (t_col < hi), 1.0, 0.0).astype(BF16)
        wsum = _dot(band, a_hi[:, sl]) + _dot(band, a_lo[:, sl])
        cnt = (jnp.minimum(t_out + win // 2, seq) - jnp.maximum(t_out - win // 2, 0)).astype(F32)
        pooled = wsum / cnt - a_cur[:, sl]
        pa_parts.append(_dot(pooled.astype(BF16), pool_w_ref[g]))
    pa = jnp.concatenate(pa_parts, axis=-1) * pool_scale_ref[...]
    pa = _dot(pa.astype(BF16), w_pa_ref[...])

    o = of_ref[...].astype(F32) + ob_ref[...].astype(F32)
    o_parts = []
    for h in range(HG_HEADS):
        oh = o[:, h * HG_HEAD_DIM:(h + 1) * HG_HEAD_DIM]
        o_parts.append(oh * lax.rsqrt(jnp.mean(oh * oh, axis=-1, keepdims=True) + NORM_EPS))
    o = jnp.concatenate(o_parts, axis=-1) * hgw_ref[...] * og_ref[...].astype(F32)
    pb = _dot(o.astype(BF16), w_pb_ref[...])

    merged = sga_ref[...].astype(F32) * pa + sgb_ref[...].astype(F32) * pb
    x1 = x_ref[...] + _dot(merged.astype(BF16), w_out_ref[...])
    x1_ref[...] = x1

    xn = _rms(x1, n2_ref[...])
    xnt_ref[...] = xn.T.astype(BF16)
    qp = _dot(xn.astype(BF16), wq_ref[...]).astype(BF16)
    for j in range(2 * PEER_HEADS):
        st_ref[j] = _dot_nt(keys_ref[j], qp[:, j * PEER_HALF:(j + 1) * PEER_HALF])


def _merge(x, a, o_f, o_b, og, sga, sgb, wts, tm):
    bsz, seq, _ = x.shape
    nt = seq // tm
    t = bsz * seq
    hb = tm // POOL_HALO
    last_halo = seq // POOL_HALO - 1
    row = lambda w: pl.BlockSpec((None, tm, w), lambda b, i: (b, i, 0))
    prev = pl.BlockSpec((None, POOL_HALO, POOL_WIDTH), lambda b, i: (b, jnp.maximum(i * hb - 1, 0), 0))
    nxt = pl.BlockSpec((None, POOL_HALO, POOL_WIDTH), lambda b, i: (b, jnp.minimum((i + 1) * hb, last_halo), 0))
    full = lambda w: pl.BlockSpec(w.shape, lambda b, i: (0,) * w.ndim)
    return pl.pallas_call(
        functools.partial(_merge_kernel, seq),
        grid=(bsz, nt),
        in_specs=[row(D_MODEL), prev, row(POOL_WIDTH), nxt] + [row(D_MODEL)] * 5 + [full(w) for w in wts],
        out_specs=(row(D_MODEL),
                   pl.BlockSpec((D_MODEL, tm), lambda b, i: (0, b * nt + i)),
                   pl.BlockSpec((2 * PEER_HEADS, N_KEYS, tm), lambda b, i: (0, 0, b * nt + i))),
        out_shape=(jax.ShapeDtypeStruct((bsz, seq, D_MODEL), F32),
                   jax.ShapeDtypeStruct((D_MODEL, t), BF16),
                   jax.ShapeDtypeStruct((2 * PEER_HEADS, N_KEYS, t), F32)),
        compiler_params=pltpu.CompilerParams(
            dimension_semantics=("parallel", "parallel"), vmem_limit_bytes=VMEM_LIMIT_BYTES),
        name="merge",
    )(x, a, a, a, o_f, o_b, og, sga, sgb, *wts)


def _sort_network(n):
    def merge(lo, hi, r):
        step = r * 2
        if step < hi - lo:
            yield from merge(lo, hi, step)
            yield from merge(lo + r, hi, step)
            yield from ((i, i + r) for i in range(lo + r, hi - r, step))
        else:
            yield (lo, lo + r)

    def sort(lo, hi):
        if hi > lo:
            mid = lo + (hi - lo) // 2
            yield from sort(lo, mid)
            yield from sort(mid + 1, hi)
            yield from merge(lo, hi, 1)

    return tuple(sort(0, n - 1))


def _exchange(v, i, j):
    v[i], v[j] = jnp.maximum(v[i], v[j]), jnp.minimum(v[i], v[j])


def _top_sorted(pieces):
    k = len(pieces)
    v = list(pieces)
    for i, j in _sort_network(k):
        _exchange(v, i, j)
    shift = SUBLANES // 2
    while shift:
        w = [pltpu.roll(x, shift, axis=0) for x in v]
        v = [jnp.maximum(v[d], w[k - 1 - d]) for d in range(k)]
        stride = k // 2
        while stride:
            for i in range(k):
                if not i & stride:
                    _exchange(v, i, i + stride)
            stride //= 2
        shift //= 2
    return v


def _sublane_sum(x):
    shift = SUBLANES // 2
    while shift:
        x = x + pltpu.roll(x, shift, axis=0)
        shift //= 2
    return x


def _sublane_max(x):
    shift = SUBLANES // 2
    while shift:
        x = jnp.maximum(x, pltpu.roll(x, shift, axis=0))
        shift //= 2
    return x


def _select_kernel(s_ref, rank2_ref, p2_ref, nsel_ref, p1_ref):
    k, half = PEER_TOPK, PEER_TOPK // 2
    assert N_KEYS == k * SUBLANES and half == SUBLANES
    s1 = [s_ref[0, d * SUBLANES:(d + 1) * SUBLANES, :] for d in range(k)]
    s2 = [s_ref[1, d * SUBLANES:(d + 1) * SUBLANES, :] for d in range(k)]
    top1 = _top_sorted(s1)
    top2 = _top_sorted(s2)
    row = lax.broadcasted_iota(jnp.int32, s1[0].shape, 0)
    a_stack = [functools.reduce(lambda acc, c: jnp.where(row == c, top1[o + c], acc), range(half), top1[o])
               for o in (0, half)]
    b_stack = [functools.reduce(lambda acc, c: jnp.where(row == c, top2[o + c], acc), range(half), top2[o])
               for o in (0, half)]
    cands = [top1[0] + b_stack[0], top1[0] + b_stack[1]]
    cands += [top1[r] + b_stack[0] for r in range(1, half)]
    cands += [a_stack[1] + top2[0]]
    rem = list(cands)
    thr = None
    for _ in range(k):
        thr = _sublane_max(functools.reduce(jnp.maximum, rem))
        rem = [jnp.where(x == thr, NEG_INF, x) for x in rem]
    best = top1[0] + top2[0]
    chosen = [x >= thr for x in cands]
    z = _sublane_sum(sum(jnp.where(ch, jnp.exp(x - best), 0.0) for ch, x in zip(chosen, cands)))
    count = [jnp.where(ch, 1.0, 0.0) for ch in chosen]
    n_sel = [_sublane_sum(count[0] + count[1])] + [_sublane_sum(count[r + 1]) for r in range(1, half)]
    last = count[half + 1]
    n_sel += [_sublane_sum(jnp.where(row == c, last, 0.0)) for c in range(half)]
    scale = 0.5 / z
    rank2, p2 = [], []
    for d in range(k):
        sl = slice(d * SUBLANES, (d + 1) * SUBLANES)
        nsel = jnp.zeros_like(s1[d])
        rank = jnp.full_like(s2[d], NOT_RANKED)
        for r in reversed(range(k)):
            nsel = jnp.where(s1[d] == top1[r], n_sel[r], nsel)
            rank = jnp.where(s2[d] == top2[r], float(r + 1), rank)
        nsel_ref[sl, :] = nsel
        p1_ref[sl, :] = jnp.exp(s1[d] - top1[0])
        rank2.append(rank)
        p2.append(jnp.exp(s2[d] - top2[0]) * scale)
    rank2_ref[...] = jnp.concatenate(rank2, axis=0).astype(rank2_ref.dtype)
    p2_ref[...] = jnp.concatenate(p2, axis=0).astype(p2_ref.dtype)


def _select(s_t, tl):
    t = s_t.shape[-1]
    s4 = s_t.reshape(PEER_HEADS, 2, N_KEYS, t)
    out = lambda dt: jax.ShapeDtypeStruct((PEER_HEADS, N_KEYS, t), dt)
    spec = pl.BlockSpec((None, N_KEYS, tl), lambda i, h: (h, 0, i))
    return pl.pallas_call(
        _select_kernel,
        grid=(t // tl, PEER_HEADS),
        in_specs=[pl.BlockSpec((None, 2, N_KEYS, tl), lambda i, h: (h, 0, 0, i))],
        out_specs=(spec,) * 4,
        out_shape=(out(BF16), out(BF16), out(F32), out(F32)),
        compiler_params=pltpu.CompilerParams(
            dimension_semantics=("parallel", "parallel"), vmem_limit_bytes=VMEM_LIMIT_BYTES),
        name="select",
    )(s4)


def _peer_gates(act, rank2_ref, p2_ref, nsel_ref, p1_ref, gate_ref, slot, cols):
    tp = act.shape[1]
    for il in range(act.shape[0] // N_KEYS):
        x = act[il * N_KEYS:(il + 1) * N_KEYS, :].astype(BF16)
        w = None
        for h in range(PEER_HEADS):
            ns = jnp.broadcast_to(nsel_ref[h, il:il + 1, cols], (N_KEYS, tp)).astype(BF16)
            p1 = jnp.broadcast_to(p1_ref[h, il:il + 1, cols], (N_KEYS, tp)).astype(BF16)
            term = jnp.where(rank2_ref[h, :, cols] <= ns, p2_ref[h, :, cols], 0) * p1
            w = term if w is None else w + term
        inner = x * (GELU_C0 + GELU_C1 * (x * x))
        gate_ref[slot, il * N_KEYS:(il + 1) * N_KEYS, cols] = (x + x * jnp.tanh(inner)) * w


def _peer_kernel(xnt_ref, u_ref, vt_ref, rank2_ref, p2_ref, nsel_ref, p1_ref, x1_ref, fnw_ref,
                 y_ref, acc_ref, gate_ref):
    c = pl.program_id(1)
    last = pl.num_programs(1) - 1
    slot = lax.rem(c, 2)
    tm = xnt_ref.shape[1]
    pieces = [slice(p * PEER_PIECE, (p + 1) * PEER_PIECE) for p in range(tm // PEER_PIECE)]

    def pre_activations():
        return [_dot(u_ref[...], xnt_ref[:, cols]) for cols in pieces]

    def v_product(prev):
        return [_dot(vt_ref[...], gate_ref[prev, :, cols]) for cols in pieces]

    def build(acts, slot):
        for act, cols in zip(acts, pieces):
            _peer_gates(act, rank2_ref, p2_ref, nsel_ref, p1_ref, gate_ref, slot, cols)

    @pl.when(c == 0)
    def _():
        acc_ref[...] = jnp.zeros_like(acc_ref)
        build(pre_activations(), 0)

    @pl.when((c > 0) & (c < last))
    def _():
        acts = pre_activations()
        for upd, cols in zip(v_product(1 - slot), pieces):
            acc_ref[:, cols] += upd
        build(acts, slot)

    @pl.when(c == last)
    def _():
        out = acc_ref[...] + jnp.concatenate(v_product(1 - slot), axis=1)
        y_ref[...] = _rms(x1_ref[...] + out.T, fnw_ref[...])


def _peer(xn_t, u, v_t, rank2, p2, nsel, p1, x1, fnw, tm):
    t = xn_t.shape[1]
    nch, _, ec = v_t.shape
    ni = ec // N_KEYS
    chunk = lambda c, lag: jnp.clip(c - lag, 0, nch - 1)
    tok3 = pl.BlockSpec((PEER_HEADS, N_KEYS, tm), lambda i, c: (0, 0, i))
    chunk3 = pl.BlockSpec((PEER_HEADS, ni, tm), lambda i, c: (0, chunk(c, 0), i))
    return pl.pallas_call(
        _peer_kernel,
        grid=(t // tm, nch + 1),
        in_specs=[pl.BlockSpec((D_MODEL, tm), lambda i, c: (0, i)),
                  pl.BlockSpec((ec, D_MODEL), lambda i, c: (chunk(c, 0), 0)),
                  pl.BlockSpec((None, D_MODEL, ec), lambda i, c: (chunk(c, 1), 0, 0)),
                  tok3, tok3, chunk3, chunk3,
                  pl.BlockSpec((tm, D_MODEL), lambda i, c: (i, 0)),
                  pl.BlockSpec((1, D_MODEL), lambda i, c: (0, 0))],
        out_specs=pl.BlockSpec((tm, D_MODEL), lambda i, c: (i, 0)),
        out_shape=jax.ShapeDtypeStruct((t, D_MODEL), F32),
        scratch_shapes=[pltpu.VMEM((D_MODEL, tm), F32), pltpu.VMEM((2, ec, tm), BF16)],
        compiler_params=pltpu.CompilerParams(
            dimension_semantics=("parallel", "arbitrary"), vmem_limit_bytes=VMEM_LIMIT_BYTES),
        name="peer",
    )(xn_t, u, v_t, rank2, p2, nsel, p1, x1, fnw)


def _tile(n, pref):
    while n % pref:
        pref //= 2
    return pref


def _trunk(x, w):
    bsz, seq, d = x.shape
    t = bsz * seq
    a, q, kf, gf, kb, gb, v, og, sga, sgb = _inproj(
        x.reshape(t, d), w["norm1"], w["w_in"], w["lbf"], w["lbb"], _tile(t, 512))
    r3 = lambda z: z.reshape(bsz, seq, z.shape[-1])
    o_f, o_b = _gla(r3(q), r3(kf), r3(gf), r3(kb), r3(gb), r3(v), _tile(seq, 128))
    x1, xn_t, s_t = _merge(x, r3(a), o_f, o_b, r3(og), r3(sga), r3(sgb), w["merge"], _tile(seq, 256))
    rank2, p2, nsel, p1 = _select(s_t, _tile(t, 512))
    y = _peer(xn_t, w["u"], w["v_t"], rank2, p2, nsel, p1, x1.reshape(t, d), w["final_norm"],
              _tile(t, 512))
    return y.reshape(bsz, seq, d)


def kernel(x_prompt, x_sample, norm1_w, w_in, pool_w, pool_scale, lb_fwd, lb_bwd, hg_norm_w, w_pa, w_pb,
           w_out, norm2_w, peer_wq, peer_keys, peer_u, peer_v, final_norm_w):
    layer = 0
    lower_bound = lambda lb: jnp.cumsum(jax.nn.softmax(lb.astype(F32), axis=0), axis=0)[layer][None, :]
    row = lambda p: p.astype(F32).reshape(1, -1)
    w = {
        "norm1": row(norm1_w[layer]),
        "w_in": w_in[layer].astype(BF16),
        "lbf": lower_bound(lb_fwd),
        "lbb": lower_bound(lb_bwd),
        "merge": (pool_w[layer].astype(BF16), row(pool_scale[layer]), row(hg_norm_w[layer]),
                  w_pa[layer].astype(BF16), w_pb[layer].astype(BF16), w_out[layer].astype(BF16),
                  row(norm2_w[layer]), peer_wq[layer].astype(BF16),
                  peer_keys[layer].astype(BF16).reshape(2 * PEER_HEADS, N_KEYS, PEER_HALF)),
        "u": peer_u[layer].astype(BF16),
        "v_t": peer_v[layer].astype(BF16).reshape(-1, PEER_CHUNK, D_MODEL).transpose(0, 2, 1),
        "final_norm": row(final_norm_w),
    }
    return _trunk(x_prompt, w), _trunk(x_sample, w)
```

```python
import functools

import jax
import jax.numpy as jnp
from jax import lax
from jax.experimental import pallas as pl
from jax.experimental.pallas import tpu as pltpu

F32 = jnp.float32
BF16 = jnp.bfloat16

D_MODEL = 1024
POOL_WINDOWS = (2, 4, 8, 16)
POOL_GROUP_WIDTH = 128
POOL_WIDTH = 512
SUBLANES = 8
POOL_HALO = SUBLANES
HG_HEADS = 8
HG_HEAD_DIM = 128
PEER_HEADS = 8
PEER_HALF = 128
N_KEYS = 128
N_EXPERTS = N_KEYS * N_KEYS
PEER_TOPK = 16
NORM_EPS = 1e-6
NOT_RANKED = 100.0
NEG_INF = float("-inf")
PEER_PIECE = 256
PEER_CHUNK = 2048
GLA_GROUP = 2
GELU_C0 = 0.7978845608028654
GELU_C1 = GELU_C0 * 0.044715
LOG2_E = 1.4426950408889634

VMEM_LIMIT_BYTES = 56 * 1024 * 1024


def _rms(x, w):
    return x * lax.rsqrt(jnp.mean(x * x, axis=-1, keepdims=True) + NORM_EPS) * w


def _dot(a, b):
    return jnp.dot(a, b, preferred_element_type=F32)


def _dot_nt(a, b):
    return lax.dot_general(a, b, (((1,), (1,)), ((), ())), preferred_element_type=F32)


def _dot_tn(a, b):
    return lax.dot_general(a, b, (((0,), (0,)), ((), ())), preferred_element_type=F32)


def _inproj_kernel(x_ref, n1_ref, w_ref, lbf_ref, lbb_ref,
                   a_ref, q_ref, kf_ref, gf_ref, kb_ref, gb_ref, v_ref, og_ref, sga_ref, sgb_ref):
    h = _rms(x_ref[...], n1_ref[...]).astype(BF16)

    def proj(block):
        c0 = POOL_WIDTH + (block - 1) * D_MODEL if block > 0 else 0
        width = D_MODEL if block > 0 else POOL_WIDTH
        return _dot(h, w_ref[:, c0:c0 + width])

    a_ref[...] = proj(0)
    qz = proj(1)
    q_ref[...] = (qz * jax.nn.sigmoid(qz)).astype(BF16)
    for block, lb_ref, k_ref, g_ref in ((2, lbf_ref, kf_ref, gf_ref), (3, lbb_ref, kb_ref, gb_ref)):
        lb = lb_ref[...]
        gap = (1.0 - lb) * jax.nn.sigmoid(proj(block))
        g_ref[...] = jnp.log(lb + gap)
        k_ref[...] = ((1.0 - lb) - gap).astype(BF16)
    v_ref[...] = proj(4).astype(BF16)
    ogz = proj(5)
    og_ref[...] = (ogz * jax.nn.sigmoid(ogz)).astype(BF16)
    sga_ref[...] = jax.nn.sigmoid(proj(6)).astype(BF16)
    sgb_ref[...] = jax.nn.sigmoid(proj(7)).astype(BF16)


def _inproj(x2, n1, w_in, lbf, lbb, tm):
    t = x2.shape[0]
    row = lambda w: pl.BlockSpec((tm, w), lambda i: (i, 0))
    full = lambda a: pl.BlockSpec(a.shape, lambda i: (0,) * a.ndim)
    wide = lambda dt: jax.ShapeDtypeStruct((t, D_MODEL), dt)
    out_shape = (jax.ShapeDtypeStruct((t, POOL_WIDTH), F32),
                 wide(BF16), wide(BF16), wide(F32), wide(BF16), wide(F32),
                 wide(BF16), wide(BF16), wide(BF16), wide(BF16))
    return pl.pallas_call(
        _inproj_kernel,
        grid=(t // tm,),
        in_specs=[row(D_MODEL), full(n1), full(w_in), full(lbf), full(lbb)],
        out_specs=(row(POOL_WIDTH),) + (row(D_MODEL),) * 9,
        out_shape=out_shape,
        compiler_params=pltpu.CompilerParams(
            dimension_semantics=("parallel",), vmem_limit_bytes=VMEM_LIMIT_BYTES),
        name="inproj",
    )(x2, n1, w_in, lbf, lbb)


def _split2(x):
    hi = x.astype(BF16)
    return hi, (x - hi.astype(F32)).astype(BF16)


def _gla_direction(q_ref, k_ref, g_ref, v_ref, o_ref, s_ref, reverse):
    c = q_ref.shape[0]
    hc = c // 2

    def travel_order(n):
        t_idx = lax.broadcasted_iota(jnp.int32, (n, n), 0)
        s_idx = lax.broadcasted_iota(jnp.int32, (n, n), 1)
        return (s_idx >= t_idx) if reverse else (s_idx <= t_idx)

    causal = travel_order(hc)
    tri = jnp.where(travel_order(c), 1.0, 0.0).astype(BF16)
    b_all = sum(_dot(tri, part) for part in _split2(g_ref[...] * LOG2_E))
    early, late = (slice(hc, c), slice(0, hc)) if reverse else (slice(0, hc), slice(hc, c))
    edge, end = (hc, 0) if reverse else (hc - 1, c - 1)
    mid_e, mid_l = early.start + hc // 2, late.start + hc // 2
    in_row_order = (lambda e, l: [l, e]) if reverse else (lambda e, l: [e, l])
    for h in range(HG_HEADS):
        sl = slice(h * HG_HEAD_DIM, (h + 1) * HG_HEAD_DIM)
        b = b_all[:, sl]
        q = q_ref[:, sl].astype(F32)
        k = k_ref[:, sl].astype(F32)
        v = v_ref[:, sl]
        row = lambda r: b[r:r + 1, :]
        q_ee = q[early] * jnp.exp2(b[early] - row(mid_e))
        k_ee = k[early] * jnp.exp2(row(mid_e) - b[early])
        q_ll = q[late] * jnp.exp2(b[late] - row(mid_l))
        k_ll = k[late] * jnp.exp2(row(mid_l) - b[late])
        q_le = q[late] * jnp.exp2(b[late] - row(edge))
        k_le = k[early] * jnp.exp2(row(edge) - b[early])

        def scores(qt, kt, masked):
            att = _dot_nt(qt.astype(BF16), kt.astype(BF16))
            return (jnp.where(causal, att, 0.0) if masked else att).astype(BF16)

        att_ee = scores(q_ee, k_ee, True)
        att_ll = scores(q_ll, k_ll, True)
        att_le = scores(q_le, k_le, False)
        st = s_ref[h]
        q_in = in_row_order(q_ee * jnp.exp2(row(mid_e)), q_le * jnp.exp2(row(edge)))
        k_out = in_row_order(k_le * jnp.exp2(row(end) - row(edge)), k_ll * jnp.exp2(row(end) - row(mid_l)))
        carried = _dot_nt(jnp.concatenate(q_in, axis=0).astype(BF16), st.astype(BF16))
        o_ref[early, sl] = (_dot(att_ee, v[early]) + carried[early]).astype(o_ref.dtype)
        o_ref[late, sl] = (_dot(att_le, v[early]) + _dot(att_ll, v[late]) + carried[late]).astype(o_ref.dtype)
        s_ref[h] = st * jnp.exp2(row(end)) + _dot_tn(v, jnp.concatenate(k_out, axis=0).astype(BF16))


def _gla_kernel(qf_ref, kf_ref, gf_ref, vf_ref, qb_ref, kb_ref, gb_ref, vb_ref,
                of_ref, ob_ref, sf_ref, sb_ref):
    @pl.when(pl.program_id(1) == 0)
    def _():
        sf_ref[...] = jnp.zeros_like(sf_ref)
        sb_ref[...] = jnp.zeros_like(sb_ref)

    for s in range(qf_ref.shape[0]):
        at = lambda *refs: [r.at[s] for r in refs]
        _gla_direction(*at(qf_ref, kf_ref, gf_ref, vf_ref, of_ref, sf_ref), reverse=False)
        _gla_direction(*at(qb_ref, kb_ref, gb_ref, vb_ref, ob_ref, sb_ref), reverse=True)


def _gla(q, kf, gf, kb, gb, v, chunk, group):
    bsz, seq, _ = q.shape
    n = seq // chunk
    fwd = pl.BlockSpec((group, chunk, D_MODEL), lambda b, i: (b, i, 0))
    bwd = pl.BlockSpec((group, chunk, D_MODEL), lambda b, i: (b, n - 1 - i, 0))
    state = pltpu.VMEM((group, HG_HEADS, HG_HEAD_DIM, HG_HEAD_DIM), F32)
    out = jax.ShapeDtypeStruct((bsz, seq, D_MODEL), BF16)
    return pl.pallas_call(
        _gla_kernel,
        grid=(bsz // group, n),
        in_specs=[fwd, fwd, fwd, fwd, bwd, bwd, bwd, bwd],
        out_specs=(fwd, bwd),
        out_shape=(out, out),
        scratch_shapes=[state, state],
        compiler_params=pltpu.CompilerParams(
            dimension_semantics=("parallel", "arbitrary"), vmem_limit_bytes=VMEM_LIMIT_BYTES),
        name="gla",
    )(q, kf, gf, v, q, kb, gb, v)


def _merge_kernel(seq, x_ref, ap_ref, a_ref, an_ref, of_ref, ob_ref, og_ref, sga_ref, sgb_ref,
                  pool_w_ref, pool_scale_ref, hgw_ref, w_pa_ref, w_pb_ref, w_out_ref,
                  n2_ref, wq_ref, keys_ref,
                  x1_ref, xnt_ref, st_ref):
    tm = x_ref.shape[0]
    ext = tm + 2 * POOL_HALO
    a_ext = jnp.concatenate([ap_ref[...], a_ref[...], an_ref[...]], axis=0)
    a_hi = a_ext.astype(BF16)
    a_lo = (a_ext - a_hi.astype(F32)).astype(BF16)
    t0 = pl.program_id(1) * tm
    t_row = t0 + lax.broadcasted_iota(jnp.int32, (tm, ext), 0)
    t_col = t0 - POOL_HALO + lax.broadcasted_iota(jnp.int32, (tm, ext), 1)
    t_out = t0 + lax.broadcasted_iota(jnp.int32, (tm, POOL_GROUP_WIDTH), 0)
    a_cur = a_ref[...]
    pa_parts = []
    for g, win in enumerate(POOL_WINDOWS):
        sl = slice(g * POOL_GROUP_WIDTH, (g + 1) * POOL_GROUP_WIDTH)
        lo = jnp.maximum(t_row - win // 2, 0)
        hi = jnp.minimum(t_row + win // 2, seq)
        band = jnp.where((t_col >= lo) & (t_col < hi), 1.0, 0.0).astype(BF16)
        wsum = _dot(band, a_hi[:, sl]) + _dot(band, a_lo[:, sl])
        cnt = (jnp.minimum(t_out + win // 2, seq) - jnp.maximum(t_out - win // 2, 0)).astype(F32)
        pooled = wsum / cnt - a_cur[:, sl]
        pa_parts.append(_dot(pooled.astype(BF16), pool_w_ref[g]))
    pa = jnp.concatenate(pa_parts, axis=-1) * pool_scale_ref[...]
    pa = _dot(pa.astype(BF16), w_pa_ref[...])

    o = of_ref[...].astype(F32) + ob_ref[...].astype(F32)
    o_parts = []
    for h in range(HG_HEADS):
        oh = o[:, h * HG_HEAD_DIM:(h + 1) * HG_HEAD_DIM]
        o_parts.append(oh * lax.rsqrt(jnp.mean(oh * oh, axis=-1, keepdims=True) + NORM_EPS))
    o = jnp.concatenate(o_parts, axis=-1) * hgw_ref[...] * og_ref[...].astype(F32)
    pb = _dot(o.astype(BF16), w_pb_ref[...])

    merged = sga_ref[...].astype(F32) * pa + sgb_ref[...].astype(F32) * pb
    x1 = x_ref[...] + _dot(merged.astype(BF16), w_out_ref[...])
    x1_ref[...] = x1

    xn = _rms(x1, n2_ref[...])
    xnt_ref[...] = xn.T.astype(BF16)
    qp = _dot(xn.astype(BF16), wq_ref[...]).astype(BF16)
    for j in range(2 * PEER_HEADS):
        st_ref[j] = _dot_nt(keys_ref[j], qp[:, j * PEER_HALF:(j + 1) * PEER_HALF])


def _merge(x, a, o_f, o_b, og, sga, sgb, wts, tm):
    bsz, seq, _ = x.shape
    nt = seq // tm
    t = bsz * seq
    hb = tm // POOL_HALO
    last_halo = seq // POOL_HALO - 1
    row = lambda w: pl.BlockSpec((None, tm, w), lambda b, i: (b, i, 0))
    prev = pl.BlockSpec((None, POOL_HALO, POOL_WIDTH), lambda b, i: (b, jnp.maximum(i * hb - 1, 0), 0))
    nxt = pl.BlockSpec((None, POOL_HALO, POOL_WIDTH), lambda b, i: (b, jnp.minimum((i + 1) * hb, last_halo), 0))
    full = lambda w: pl.BlockSpec(w.shape, lambda b, i: (0,) * w.ndim)
    return pl.pallas_call(
        functools.partial(_merge_kernel, seq),
        grid=(bsz, nt),
        in_specs=[row(D_MODEL), prev, row(POOL_WIDTH), nxt] + [row(D_MODEL)] * 5 + [full(w) for w in wts],
        out_specs=(row(D_MODEL),
                   pl.BlockSpec((D_MODEL, tm), lambda b, i: (0, b * nt + i)),
                   pl.BlockSpec((2 * PEER_HEADS, N_KEYS, tm), lambda b, i: (0, 0, b * nt + i))),
        out_shape=(jax.ShapeDtypeStruct((bsz, seq, D_MODEL), F32),
                   jax.ShapeDtypeStruct((D_MODEL, t), BF16),
                   jax.ShapeDtypeStruct((2 * PEER_HEADS, N_KEYS, t), F32)),
        compiler_params=pltpu.CompilerParams(
            dimension_semantics=("parallel", "parallel"), vmem_limit_bytes=VMEM_LIMIT_BYTES),
        name="merge",
    )(x, a, a, a, o_f, o_b, og, sga, sgb, *wts)


def _sort_network(n):
    def merge(lo, hi, r):
        step = r * 2
        if step < hi - lo:
            yield from merge(lo, hi, step)
            yield from merge(lo + r, hi, step)
            yield from ((i, i + r) for i in range(lo + r, hi - r, step))
        else:
            yield (lo, lo + r)

    def sort(lo, hi):
        if hi > lo:
            mid = lo + (hi - lo) // 2
            yield from sort(lo, mid)
            yield from sort(mid + 1, hi)
            yield from merge(lo, hi, 1)

    return tuple(sort(0, n - 1))


def _exchange(v, i, j):
    v[i], v[j] = jnp.maximum(v[i], v[j]), jnp.minimum(v[i], v[j])


def _top_sorted(pieces):
    k = len(pieces)
    v = list(pieces)
    for i, j in _sort_network(k):
        _exchange(v, i, j)
    shift = SUBLANES // 2
    while shift:
        w = [pltpu.roll(x, shift, axis=0) for x in v]
        v = [jnp.maximum(v[d], w[k - 1 - d]) for d in range(k)]
        stride = k // 2
        while stride:
            for i in range(k):
                if not i & stride:
                    _exchange(v, i, i + stride)
            stride //= 2
        shift //= 2
    return v


def _sublane_sum(x):
    shift = SUBLANES // 2
    while shift:
        x = x + pltpu.roll(x, shift, axis=0)
        shift //= 2
    return x


def _sublane_max(x):
    shift = SUBLANES // 2
    while shift:
        x = jnp.maximum(x, pltpu.roll(x, shift, axis=0))
        shift //= 2
    return x


def _select_kernel(s_ref, rank2_ref, p2_ref, nsel_ref, p1_ref):
    k, half = PEER_TOPK, PEER_TOPK // 2
    assert N_KEYS == k * SUBLANES and half == SUBLANES
    s1 = [s_ref[0, d * SUBLANES:(d + 1) * SUBLANES, :] for d in range(k)]
    s2 = [s_ref[1, d * SUBLANES:(d + 1) * SUBLANES, :] for d in range(k)]
    top1 = _top_sorted(s1)
    top2 = _top_sorted(s2)
    row = lax.broadcasted_iota(jnp.int32, s1[0].shape, 0)
    a_stack = [functools.reduce(lambda acc, c: jnp.where(row == c, top1[o + c], acc), range(half), top1[o])
               for o in (0, half)]
    b_stack = [functools.reduce(lambda acc, c: jnp.where(row == c, top2[o + c], acc), range(half), top2[o])
               for o in (0, half)]
    cands = [top1[0] + b_stack[0], top1[0] + b_stack[1]]
    cands += [top1[r] + b_stack[0] for r in range(1, half)]
    cands += [a_stack[1] + top2[0]]
    rem = list(cands)
    thr = None
    for _ in range(k):
        thr = _sublane_max(functools.reduce(jnp.maximum, rem))
        rem = [jnp.where(x == thr, NEG_INF, x) for x in rem]
    best = top1[0] + top2[0]
    chosen = [x >= thr for x in cands]
    z = _sublane_sum(sum(jnp.where(ch, jnp.exp(x - best), 0.0) for ch, x in zip(chosen, cands)))
    count = [jnp.where(ch, 1.0, 0.0) for ch in chosen]
    n_sel = [_sublane_sum(count[0] + count[1])] + [_sublane_sum(count[r + 1]) for r in range(1, half)]
    last = count[half + 1]
    n_sel += [_sublane_sum(jnp.where(row == c, last, 0.0)) for c in range(half)]
    scale = 0.5 / z
    rank2, p2 = [], []
    for d in range(k):
        sl = slice(d * SUBLANES, (d + 1) * SUBLANES)
        nsel = jnp.zeros_like(s1[d])
        rank = jnp.full_like(s2[d], NOT_RANKED)
        for r in reversed(range(k)):
            nsel = jnp.where(s1[d] == top1[r], n_sel[r], nsel)
            rank = jnp.where(s2[d] == top2[r], float(r + 1), rank)
        nsel_ref[sl, :] = nsel
        p1_ref[sl, :] = jnp.exp(s1[d] - top1[0])
        rank2.append(rank)
        p2.append(jnp.exp(s2[d] - top2[0]) * scale)
    rank2_ref[...] = jnp.concatenate(rank2, axis=0).astype(rank2_ref.dtype)
    p2_ref[...] = jnp.concatenate(p2, axis=0).astype(p2_ref.dtype)


def _select(s_t, tl):
    t = s_t.shape[-1]
    s4 = s_t.reshape(PEER_HEADS, 2, N_KEYS, t)
    out = lambda dt: jax.ShapeDtypeStruct((PEER_HEADS, N_KEYS, t), dt)
    spec = pl.BlockSpec((None, N_KEYS, tl), lambda i, h: (h, 0, i))
    return pl.pallas_call(
        _select_kernel,
        grid=(t // tl, PEER_HEADS),
        in_specs=[pl.BlockSpec((None, 2, N_KEYS, tl), lambda i, h: (h, 0, 0, i))],
        out_specs=(spec,) * 4,
        out_shape=(out(BF16), out(BF16), out(F32), out(F32)),
        compiler_params=pltpu.CompilerParams(
            dimension_semantics=("parallel", "parallel"), vmem_limit_bytes=VMEM_LIMIT_BYTES),
        name="select",
    )(s4)


def _peer_gates(act, rank2_ref, p2_ref, nsel_ref, p1_ref, gate_ref, slot, cols):
    tp = act.shape[1]
    for il in range(act.shape[0] // N_KEYS):
        x = act[il * N_KEYS:(il + 1) * N_KEYS, :].astype(BF16)
        w = None
        for h in range(PEER_HEADS):
            ns = jnp.broadcast_to(nsel_ref[h, il:il + 1, cols], (N_KEYS, tp)).astype(BF16)
            p1 = jnp.broadcast_to(p1_ref[h, il:il + 1, cols], (N_KEYS, tp)).astype(BF16)
            term = jnp.where(rank2_ref[h, :, cols] <= ns, p2_ref[h, :, cols], 0) * p1
            w = term if w is None else w + term
        inner = x * (GELU_C0 + GELU_C1 * (x * x))
        gate_ref[slot, il * N_KEYS:(il + 1) * N_KEYS, cols] = (x + x * jnp.tanh(inner)) * w


def _peer_kernel(xnt_ref, u_ref, vt_ref, rank2_ref, p2_ref, nsel_ref, p1_ref, x1_ref, fnw_ref,
                 y_ref, acc_ref, gate_ref):
    c = pl.program_id(1)
    last = pl.num_programs(1) - 1
    slot = lax.rem(c, 2)
    tm = xnt_ref.shape[1]
    pieces = [slice(p * PEER_PIECE, (p + 1) * PEER_PIECE) for p in range(tm // PEER_PIECE)]

    def pre_activations():
        return [_dot(u_ref[...], xnt_ref[:, cols]) for cols in pieces]

    def v_product(prev):
        return [_dot(vt_ref[...], gate_ref[prev, :, cols]) for cols in pieces]

    def build(acts, slot):
        for act, cols in zip(acts, pieces):
            _peer_gates(act, rank2_ref, p2_ref, nsel_ref, p1_ref, gate_ref, slot, cols)

    @pl.when(c == 0)
    def _():
        acc_ref[...] = jnp.zeros_like(acc_ref)
        build(pre_activations(), 0)

    @pl.when((c > 0) & (c < last))
    def _():
        acts = pre_activations()
        for upd, cols in zip(v_product(1 - slot), pieces):
            acc_ref[:, cols] += upd
        build(acts, slot)

    @pl.when(c == last)
    def _():
        out = acc_ref[...] + jnp.concatenate(v_product(1 - slot), axis=1)
        y_ref[...] = _rms(x1_ref[...] + out.T, fnw_ref[...])


def _peer(xn_t, u, v_t, rank2, p2, nsel, p1, x1, fnw, tm):
    t = xn_t.shape[1]
    nch, _, ec = v_t.shape
    ni = ec // N_KEYS
    chunk = lambda c, lag: jnp.clip(c - lag, 0, nch - 1)
    tok3 = pl.BlockSpec((PEER_HEADS, N_KEYS, tm), lambda i, c: (0, 0, i))
    chunk3 = pl.BlockSpec((PEER_HEADS, ni, tm), lambda i, c: (0, chunk(c, 0), i))
    return pl.pallas_call(
        _peer_kernel,
        grid=(t // tm, nch + 1),
        in_specs=[pl.BlockSpec((D_MODEL, tm), lambda i, c: (0, i)),
                  pl.BlockSpec((ec, D_MODEL), lambda i, c: (chunk(c, 0), 0)),
                  pl.BlockSpec((None, D_MODEL, ec), lambda i, c: (chunk(c, 1), 0, 0)),
                  tok3, tok3, chunk3, chunk3,
                  pl.BlockSpec((tm, D_MODEL), lambda i, c: (i, 0)),
                  pl.BlockSpec((1, D_MODEL), lambda i, c: (0, 0))],
        out_specs=pl.BlockSpec((tm, D_MODEL), lambda i, c: (i, 0)),
        out_shape=jax.ShapeDtypeStruct((t, D_MODEL), F32),
        scratch_shapes=[pltpu.VMEM((D_MODEL, tm), F32), pltpu.VMEM((2, ec, tm), BF16)],
        compiler_params=pltpu.CompilerParams(
            dimension_semantics=("parallel", "arbitrary"), vmem_limit_bytes=VMEM_LIMIT_BYTES),
        name="peer",
    )(xn_t, u, v_t, rank2, p2, nsel, p1, x1, fnw)


def _tile(n, pref):
    while n % pref:
        pref //= 2
    return pref


def _trunk(x, w):
    bsz, seq, d = x.shape
    t = bsz * seq
    a, q, kf, gf, kb, gb, v, og, sga, sgb = _inproj(
        x.reshape(t, d), w["norm1"], w["w_in"], w["lbf"], w["lbb"], _tile(t, 512))
    r3 = lambda z: z.reshape(bsz, seq, z.shape[-1])
    o_f, o_b = _gla(r3(q), r3(kf), r3(gf), r3(kb), r3(gb), r3(v), _tile(seq, 128), _tile(bsz, GLA_GROUP))
    x1, xn_t, s_t = _merge(x, r3(a), o_f, o_b, r3(og), r3(sga), r3(sgb), w["merge"], _tile(seq, 256))
    rank2, p2, nsel, p1 = _select(s_t, _tile(t, 512))
    y = _peer(xn_t, w["u"], w["v_t"], rank2, p2, nsel, p1, x1.reshape(t, d), w["final_norm"],
              _tile(t, 512))
    return y.reshape(bsz, seq, d)


def kernel(x_prompt, x_sample, norm1_w, w_in, pool_w, pool_scale, lb_fwd, lb_bwd, hg_norm_w, w_pa, w_pb,
           w_out, norm2_w, peer_wq, peer_keys, peer_u, peer_v, final_norm_w):
    layer = 0
    lower_bound = lambda lb: jnp.cumsum(jax.nn.softmax(lb.astype(F32), axis=0), axis=0)[layer][None, :]
    row = lambda p: p.astype(F32).reshape(1, -1)
    w = {
        "norm1": row(norm1_w[layer]),
        "w_in": w_in[layer].astype(BF16),
        "lbf": lower_bound(lb_fwd),
        "lbb": lower_bound(lb_bwd),
        "merge": (pool_w[layer].astype(BF16), row(pool_scale[layer]), row(hg_norm_w[layer]),
                  w_pa[layer].astype(BF16), w_pb[layer].astype(BF16), w_out[layer].astype(BF16),
                  row(norm2_w[layer]), peer_wq[layer].astype(BF16),
                  peer_keys[layer].astype(BF16).reshape(2 * PEER_HEADS, N_KEYS, PEER_HALF)),
        "u": peer_u[layer].astype(BF16),
        "v_t": peer_v[layer].astype(BF16).reshape(-1, PEER_CHUNK, D_MODEL).transpose(0, 2, 1),
        "final_norm": row(final_norm_w),
    }
    return _trunk(x_prompt, w), _trunk(x_sample, w)
```

```python
import functools

import jax
import jax.numpy as jnp
from jax import lax
from jax.experimental import pallas as pl
from jax.experimental.pallas import tpu as pltpu

F32 = jnp.float32
BF16 = jnp.bfloat16

D_MODEL = 1024
POOL_WINDOWS = (2, 4, 8, 16)
POOL_GROUP_WIDTH = 128
POOL_WIDTH = 512
SUBLANES = 8
POOL_HALO = SUBLANES
HG_HEADS = 8
HG_HEAD_DIM = 128
PEER_HEADS = 8
PEER_HALF = 128
N_KEYS = 128
N_EXPERTS = N_KEYS * N_KEYS
PEER_TOPK = 16
NORM_EPS = 1e-6
NOT_RANKED = 100.0
PEER_PIECE = 256
PEER_CHUNK = 2048
GLA_GROUP = 4
GELU_C0 = 0.7978845608028654
GELU_C1 = GELU_C0 * 0.044715
LOG2_E = 1.4426950408889634

VMEM_LIMIT_BYTES = 56 * 1024 * 1024


def _rms(x, w):
    return x * lax.rsqrt(jnp.mean(x * x, axis=-1, keepdims=True) + NORM_EPS) * w


def _dot(a, b):
    return jnp.dot(a, b, preferred_element_type=F32)


def _dot_nt(a, b):
    return lax.dot_general(a, b, (((1,), (1,)), ((), ())), preferred_element_type=F32)


def _dot_tn(a, b):
    return lax.dot_general(a, b, (((0,), (0,)), ((), ())), preferred_element_type=F32)


def _inproj_kernel(x_ref, n1_ref, w_ref, lbf_ref, lbb_ref,
                   a_ref, q_ref, kf_ref, gf_ref, kb_ref, gb_ref, v_ref, og_ref, sga_ref, sgb_ref):
    h = _rms(x_ref[...], n1_ref[...]).astype(BF16)

    def proj(block):
        c0 = POOL_WIDTH + (block - 1) * D_MODEL if block > 0 else 0
        width = D_MODEL if block > 0 else POOL_WIDTH
        return _dot(h, w_ref[:, c0:c0 + width])

    a_ref[...] = proj(0)
    qz = proj(1)
    q_ref[...] = (qz * jax.nn.sigmoid(qz)).astype(BF16)
    for block, lb_ref, k_ref, g_ref in ((2, lbf_ref, kf_ref, gf_ref), (3, lbb_ref, kb_ref, gb_ref)):
        lb = lb_ref[...]
        gap = (1.0 - lb) * jax.nn.sigmoid(proj(block))
        g_ref[...] = jnp.log(lb + gap)
        k_ref[...] = ((1.0 - lb) - gap).astype(BF16)
    v_ref[...] = proj(4).astype(BF16)
    ogz = proj(5)
    og_ref[...] = (ogz * jax.nn.sigmoid(ogz)).astype(BF16)
    sga_ref[...] = jax.nn.sigmoid(proj(6)).astype(BF16)
    sgb_ref[...] = jax.nn.sigmoid(proj(7)).astype(BF16)


def _inproj(x2, n1, w_in, lbf, lbb, tm):
    t = x2.shape[0]
    row = lambda w: pl.BlockSpec((tm, w), lambda i: (i, 0))
    full = lambda a: pl.BlockSpec(a.shape, lambda i: (0,) * a.ndim)
    wide = lambda dt: jax.ShapeDtypeStruct((t, D_MODEL), dt)
    out_shape = (jax.ShapeDtypeStruct((t, POOL_WIDTH), F32),
                 wide(BF16), wide(BF16), wide(F32), wide(BF16), wide(F32),
                 wide(BF16), wide(BF16), wide(BF16), wide(BF16))
    return pl.pallas_call(
        _inproj_kernel,
        grid=(t // tm,),
        in_specs=[row(D_MODEL), full(n1), full(w_in), full(lbf), full(lbb)],
        out_specs=(row(POOL_WIDTH),) + (row(D_MODEL),) * 9,
        out_shape=out_shape,
        compiler_params=pltpu.CompilerParams(
            dimension_semantics=("parallel",), vmem_limit_bytes=VMEM_LIMIT_BYTES),
        name="inproj",
    )(x2, n1, w_in, lbf, lbb)


def _split2(x):
    hi = x.astype(BF16)
    return hi, (x - hi.astype(F32)).astype(BF16)


def _gla_direction(q_ref, k_ref, g_ref, v_ref, o_ref, s_ref, reverse):
    c = q_ref.shape[0]
    hc = c // 2

    def travel_order(n):
        t_idx = lax.broadcasted_iota(jnp.int32, (n, n), 0)
        s_idx = lax.broadcasted_iota(jnp.int32, (n, n), 1)
        return (s_idx >= t_idx) if reverse else (s_idx <= t_idx)

    causal = travel_order(hc)
    tri = jnp.where(travel_order(c), 1.0, 0.0).astype(BF16)
    b_all = sum(_dot(tri, part) for part in _split2(g_ref[...] * LOG2_E))
    early, late = (slice(hc, c), slice(0, hc)) if reverse else (slice(0, hc), slice(hc, c))
    edge, end = (hc, 0) if reverse else (hc - 1, c - 1)
    mid_e, mid_l = early.start + hc // 2, late.start + hc // 2
    in_row_order = (lambda e, l: [l, e]) if reverse else (lambda e, l: [e, l])
    for h in range(HG_HEADS):
        sl = slice(h * HG_HEAD_DIM, (h + 1) * HG_HEAD_DIM)
        b = b_all[:, sl]
        q = q_ref[:, sl].astype(F32)
        k = k_ref[:, sl].astype(F32)
        v = v_ref[:, sl]
        row = lambda r: b[r:r + 1, :]
        q_ee = q[early] * jnp.exp2(b[early] - row(mid_e))
        k_ee = k[early] * jnp.exp2(row(mid_e) - b[early])
        q_ll = q[late] * jnp.exp2(b[late] - row(mid_l))
        k_ll = k[late] * jnp.exp2(row(mid_l) - b[late])
        q_le = q[late] * jnp.exp2(b[late] - row(edge))
        k_le = k[early] * jnp.exp2(row(edge) - b[early])

        def scores(qt, kt, masked):
            att = _dot_nt(qt.astype(BF16), kt.astype(BF16))
            return (jnp.where(causal, att, 0.0) if masked else att).astype(BF16)

        att_ee = scores(q_ee, k_ee, True)
        att_ll = scores(q_ll, k_ll, True)
        att_le = scores(q_le, k_le, False)
        st = s_ref[h]
        q_in = in_row_order(q_ee * jnp.exp2(row(mid_e)), q_le * jnp.exp2(row(edge)))
        k_out = in_row_order(k_le * jnp.exp2(row(end) - row(edge)), k_ll * jnp.exp2(row(end) - row(mid_l)))
        carried = _dot_nt(jnp.concatenate(q_in, axis=0).astype(BF16), st.astype(BF16))
        o_ref[early, sl] = (_dot(att_ee, v[early]) + carried[early]).astype(o_ref.dtype)
        o_ref[late, sl] = (_dot(att_le, v[early]) + _dot(att_ll, v[late]) + carried[late]).astype(o_ref.dtype)
        s_ref[h] = st * jnp.exp2(row(end)) + _dot_tn(v, jnp.concatenate(k_out, axis=0).astype(BF16))


def _gla_kernel(qf_ref, kf_ref, gf_ref, vf_ref, qb_ref, kb_ref, gb_ref, vb_ref,
                of_ref, ob_ref, sf_ref, sb_ref):
    @pl.when(pl.program_id(1) == 0)
    def _():
        sf_ref[...] = jnp.zeros_like(sf_ref)
        sb_ref[...] = jnp.zeros_like(sb_ref)

    for s in range(qf_ref.shape[0]):
        at = lambda *refs: [r.at[s] for r in refs]
        _gla_direction(*at(qf_ref, kf_ref, gf_ref, vf_ref, of_ref, sf_ref), reverse=False)
        _gla_direction(*at(qb_ref, kb_ref, gb_ref, vb_ref, ob_ref, sb_ref), reverse=True)


def _gla(q, kf, gf, kb, gb, v, chunk, group):
    bsz, seq, _ = q.shape
    n = seq // chunk
    fwd = pl.BlockSpec((group, chunk, D_MODEL), lambda b, i: (b, i, 0))
    bwd = pl.BlockSpec((group, chunk, D_MODEL), lambda b, i: (b, n - 1 - i, 0))
    state = pltpu.VMEM((group, HG_HEADS, HG_HEAD_DIM, HG_HEAD_DIM), F32)
    out = jax.ShapeDtypeStruct((bsz, seq, D_MODEL), BF16)
    return pl.pallas_call(
        _gla_kernel,
        grid=(bsz // group, n),
        in_specs=[fwd, fwd, fwd, fwd, bwd, bwd, bwd, bwd],
        out_specs=(fwd, bwd),
        out_shape=(out, out),
        scratch_shapes=[state, state],
        compiler_params=pltpu.CompilerParams(
            dimension_semantics=("parallel", "arbitrary"), vmem_limit_bytes=VMEM_LIMIT_BYTES),
        name="gla",
    )(q, kf, gf, v, q, kb, gb, v)


def _merge_kernel(seq, x_ref, ap_ref, a_ref, an_ref, of_ref, ob_ref, og_ref, sga_ref, sgb_ref,
                  pool_w_ref, pool_scale_ref, hgw_ref, w_pa_ref, w_pb_ref, w_out_ref,
                  n2_ref, wq_ref, keys_ref,
                  x1_ref, xnt_ref, st_ref):
    tm = x_ref.shape[0]
    ext = tm + 2 * POOL_HALO
    a_ext = jnp.concatenate([ap_ref[...], a_ref[...], an_ref[...]], axis=0)
    a_hi = a_ext.astype(BF16)
    a_lo = (a_ext - a_hi.astype(F32)).astype(BF16)
    t0 = pl.program_id(1) * tm
    t_row = t0 + lax.broadcasted_iota(jnp.int32, (tm, ext), 0)
    t_col = t0 - POOL_HALO + lax.broadcasted_iota(jnp.int32, (tm, ext), 1)
    t_out = t0 + lax.broadcasted_iota(jnp.int32, (tm, POOL_GROUP_WIDTH), 0)
    a_cur = a_ref[...]
    pa_parts = []
    for g, win in enumerate(POOL_WINDOWS):
        sl = slice(g * POOL_GROUP_WIDTH, (g + 1) * POOL_GROUP_WIDTH)
        lo = jnp.maximum(t_row - win // 2, 0)
        hi = jnp.minimum(t_row + win // 2, seq)
        band = jnp.where((t_col >= lo) & (t_col < hi), 1.0, 0.0).astype(BF16)
        wsum = _dot(band, a_hi[:, sl]) + _dot(band, a_lo[:, sl])
        cnt = (jnp.minimum(t_out + win // 2, seq) - jnp.maximum(t_out - win // 2, 0)).astype(F32)
        pooled = wsum / cnt - a_cur[:, sl]
        pa_parts.append(_dot(pooled.astype(BF16), pool_w_ref[g]))
    pa = jnp.concatenate(pa_parts, axis=-1) * pool_scale_ref[...]
    pa = _dot(pa.astype(BF16), w_pa_ref[...])

    o = of_ref[...].astype(F32) + ob_ref[...].astype(F32)
    o_parts = []
    for h in range(HG_HEADS):
        oh = o[:, h * HG_HEAD_DIM:(h + 1) * HG_HEAD_DIM]
        o_parts.append(oh * lax.rsqrt(jnp.mean(oh * oh, axis=-1, keepdims=True) + NORM_EPS))
    o = jnp.concatenate(o_parts, axis=-1) * hgw_ref[...] * og_ref[...].astype(F32)
    pb = _dot(o.astype(BF16), w_pb_ref[...])

    merged = sga_ref[...].astype(F32) * pa + sgb_ref[...].astype(F32) * pb
    x1 = x_ref[...] + _dot(merged.astype(BF16), w_out_ref[...])
    x1_ref[...] = x1

    xn = _rms(x1, n2_ref[...])
    xnt_ref[...] = xn.T.astype(BF16)
    qp = _dot(xn.astype(BF16), wq_ref[...]).astype(BF16)
    for j in range(2 * PEER_HEADS):
        st_ref[j] = _dot_nt(keys_ref[j], qp[:, j * PEER_HALF:(j + 1) * PEER_HALF])


def _merge(x, a, o_f, o_b, og, sga, sgb, wts, tm):
    bsz, seq, _ = x.shape
    nt = seq // tm
    t = bsz * seq
    hb = tm // POOL_HALO
    last_halo = seq // POOL_HALO - 1
    row = lambda w: pl.BlockSpec((None, tm, w), lambda b, i: (b, i, 0))
    prev = pl.BlockSpec((None, POOL_HALO, POOL_WIDTH), lambda b, i: (b, jnp.maximum(i * hb - 1, 0), 0))
    nxt = pl.BlockSpec((None, POOL_HALO, POOL_WIDTH), lambda b, i: (b, jnp.minimum((i + 1) * hb, last_halo), 0))
    full = lambda w: pl.BlockSpec(w.shape, lambda b, i: (0,) * w.ndim)
    return pl.pallas_call(
        functools.partial(_merge_kernel, seq),
        grid=(bsz, nt),
        in_specs=[row(D_MODEL), prev, row(POOL_WIDTH), nxt] + [row(D_MODEL)] * 5 + [full(w) for w in wts],
        out_specs=(row(D_MODEL),
                   pl.BlockSpec((D_MODEL, tm), lambda b, i: (0, b * nt + i)),
                   pl.BlockSpec((2 * PEER_HEADS, N_KEYS, tm), lambda b, i: (0, 0, b * nt + i))),
        out_shape=(jax.ShapeDtypeStruct((bsz, seq, D_MODEL), F32),
                   jax.ShapeDtypeStruct((D_MODEL, t), BF16),
                   jax.ShapeDtypeStruct((2 * PEER_HEADS, N_KEYS, t), F32)),
        compiler_params=pltpu.CompilerParams(
            dimension_semantics=("parallel", "parallel"), vmem_limit_bytes=VMEM_LIMIT_BYTES),
        name="merge",
    )(x, a, a, a, o_f, o_b, og, sga, sgb, *wts)


def _sort_network(n):
    def merge(lo, hi, r):
        step = r * 2
        if step < hi - lo:
            yield from merge(lo, hi, step)
            yield from merge(lo + r, hi, step)
            yield from ((i, i + r) for i in range(lo + r, hi - r, step))
        else:
            yield (lo, lo + r)

    def sort(lo, hi):
        if hi > lo:
            mid = lo + (hi - lo) // 2
            yield from sort(lo, mid)
            yield from sort(mid + 1, hi)
            yield from merge(lo, hi, 1)

    return tuple(sort(0, n - 1))


def _exchange(v, i, j):
    if v[i] is None:
        v[i], v[j] = v[j], None
    elif v[j] is not None:
        v[i], v[j] = jnp.maximum(v[i], v[j]), jnp.minimum(v[i], v[j])


def _top_sorted(pieces):
    k = PEER_TOPK
    v = list(pieces) + [None] * (k - len(pieces))
    for i, j in _sort_network(k):
        _exchange(v, i, j)
    both = lambda a, b: b if a is None else a if b is None else jnp.maximum(a, b)
    shift = SUBLANES // 2
    while shift:
        w = [None if x is None else pltpu.roll(x, shift, axis=0) for x in v]
        v = [both(v[d], w[k - 1 - d]) for d in range(k)]
        stride = k // 2
        while stride:
            for i in range(k):
                if not i & stride:
                    _exchange(v, i, i + stride)
            stride //= 2
        shift //= 2
    return v


def _sublane_sum(x):
    shift = SUBLANES // 2
    while shift:
        x = x + pltpu.roll(x, shift, axis=0)
        shift //= 2
    return x


def _select_kernel(s_ref, rank2_ref, p2_ref, nsel_ref, p1_ref):
    k, half = PEER_TOPK, PEER_TOPK // 2
    assert N_KEYS == k * SUBLANES and half == SUBLANES
    s1 = [s_ref[0, d * SUBLANES:(d + 1) * SUBLANES, :] for d in range(k)]
    s2 = [s_ref[1, d * SUBLANES:(d + 1) * SUBLANES, :] for d in range(k)]
    top1 = _top_sorted(s1)
    top2 = _top_sorted(s2)
    row = lax.broadcasted_iota(jnp.int32, s1[0].shape, 0)
    a_stack = [functools.reduce(lambda acc, c: jnp.where(row == c, top1[o + c], acc), range(half), top1[o])
               for o in (0, half)]
    b_stack = [functools.reduce(lambda acc, c: jnp.where(row == c, top2[o + c], acc), range(half), top2[o])
               for o in (0, half)]
    cands = [top1[0] + b_stack[0], top1[0] + b_stack[1]]
    cands += [top1[r] + b_stack[0] for r in range(1, half)]
    cands += [a_stack[1] + top2[0]]
    thr = _top_sorted(cands)[k - 1]
    best = top1[0] + top2[0]
    chosen = [x >= thr for x in cands]
    z = _sublane_sum(sum(jnp.where(ch, jnp.exp(x - best), 0.0) for ch, x in zip(chosen, cands)))
    count = [jnp.where(ch, 1.0, 0.0) for ch in chosen]
    n_sel = [_sublane_sum(count[0] + count[1])] + [_sublane_sum(count[r + 1]) for r in range(1, half)]
    last = count[half + 1]
    n_sel += [_sublane_sum(jnp.where(row == c, last, 0.0)) for c in range(half)]
    scale = 0.5 / z
    rank2, p2, nsel_out, p1_out = [], [], [], []
    for d in range(k):
        nsel = jnp.zeros_like(s1[d])
        rank = jnp.full_like(s2[d], NOT_RANKED)
        for r in reversed(range(k)):
            nsel = jnp.where(s1[d] == top1[r], n_sel[r], nsel)
            rank = jnp.where(s2[d] == top2[r], float(r + 1), rank)
        nsel_out.append(nsel)
        p1_out.append(jnp.exp(s1[d] - top1[0]))
        rank2.append(rank)
        p2.append(jnp.exp(s2[d] - top2[0]) * scale)
    for ref, parts in ((rank2_ref, rank2), (p2_ref, p2), (nsel_ref, nsel_out), (p1_ref, p1_out)):
        ref[...] = jnp.concatenate(parts, axis=0).astype(ref.dtype)


def _select(s_t, tl):
    t = s_t.shape[-1]
    s4 = s_t.reshape(PEER_HEADS, 2, N_KEYS, t)
    out = lambda dt: jax.ShapeDtypeStruct((PEER_HEADS, N_KEYS, t), dt)
    spec = pl.BlockSpec((None, N_KEYS, tl), lambda i, h: (h, 0, i))
    return pl.pallas_call(
        _select_kernel,
        grid=(t // tl, PEER_HEADS),
        in_specs=[pl.BlockSpec((None, 2, N_KEYS, tl), lambda i, h: (h, 0, 0, i))],
        out_specs=(spec,) * 4,
        out_shape=(out(BF16), out(BF16), out(F32), out(F32)),
        compiler_params=pltpu.CompilerParams(
            dimension_semantics=("parallel", "parallel"), vmem_limit_bytes=VMEM_LIMIT_BYTES),
        name="select",
    )(s4)


def _peer_gates(act, rank2_ref, p2_ref, nsel_ref, p1_ref, gate_ref, slot, cols):
    tp = act.shape[1]
    for il in range(act.shape[0] // N_KEYS):
        x = act[il * N_KEYS:(il + 1) * N_KEYS, :].astype(BF16)
        w = None
        for h in range(PEER_HEADS):
            ns = jnp.broadcast_to(nsel_ref[h, il:il + 1, cols], (N_KEYS, tp)).astype(BF16)
            p1 = jnp.broadcast_to(p1_ref[h, il:il + 1, cols], (N_KEYS, tp)).astype(BF16)
            term = jnp.where(rank2_ref[h, :, cols] <= ns, p2_ref[h, :, cols], 0) * p1
            w = term if w is None else w + term
        inner = x * (GELU_C0 + GELU_C1 * (x * x))
        gate_ref[slot, il * N_KEYS:(il + 1) * N_KEYS, cols] = (x + x * jnp.tanh(inner)) * w


def _peer_kernel(xnt_ref, u_ref, vt_ref, rank2_ref, p2_ref, nsel_ref, p1_ref, x1_ref, fnw_ref,
                 y_ref, acc_ref, gate_ref):
    c = pl.program_id(1)
    last = pl.num_programs(1) - 1
    slot = lax.rem(c, 2)
    tm = xnt_ref.shape[1]
    pieces = [slice(p * PEER_PIECE, (p + 1) * PEER_PIECE) for p in range(tm // PEER_PIECE)]

    def pre_activations():
        return [_dot(u_ref[...], xnt_ref[:, cols]) for cols in pieces]

    def v_product(prev):
        return [_dot(vt_ref[...], gate_ref[prev, :, cols]) for cols in pieces]

    def build(acts, slot):
        for act, cols in zip(acts, pieces):
            _peer_gates(act, rank2_ref, p2_ref, nsel_ref, p1_ref, gate_ref, slot, cols)

    @pl.when(c == 0)
    def _():
        acc_ref[...] = jnp.zeros_like(acc_ref)
        build(pre_activations(), 0)

    @pl.when((c > 0) & (c < last))
    def _():
        acts = pre_activations()
        for upd, cols in zip(v_product(1 - slot), pieces):
            acc_ref[:, cols] += upd
        build(acts, slot)

    @pl.when(c == last)
    def _():
        out = acc_ref[...] + jnp.concatenate(v_product(1 - slot), axis=1)
        y_ref[...] = _rms(x1_ref[...] + out.T, fnw_ref[...])


def _peer(xn_t, u, v_t, rank2, p2, nsel, p1, x1, fnw, tm):
    t = xn_t.shape[1]
    nch, _, ec = v_t.shape
    ni = ec // N_KEYS
    chunk = lambda c, lag: jnp.clip(c - lag, 0, nch - 1)
    tok3 = pl.BlockSpec((PEER_HEADS, N_KEYS, tm), lambda i, c: (0, 0, i))
    chunk3 = pl.BlockSpec((PEER_HEADS, ni, tm), lambda i, c: (0, chunk(c, 0), i))
    return pl.pallas_call(
        _peer_kernel,
        grid=(t // tm, nch + 1),
        in_specs=[pl.BlockSpec((D_MODEL, tm), lambda i, c: (0, i)),
                  pl.BlockSpec((ec, D_MODEL), lambda i, c: (chunk(c, 0), 0)),
                  pl.BlockSpec((None, D_MODEL, ec), lambda i, c: (chunk(c, 1), 0, 0)),
                  tok3, tok3, chunk3, chunk3,
                  pl.BlockSpec((tm, D_MODEL), lambda i, c: (i, 0)),
                  pl.BlockSpec((1, D_MODEL), lambda i, c: (0, 0))],
        out_specs=pl.BlockSpec((tm, D_MODEL), lambda i, c: (i, 0)),
        out_shape=jax.ShapeDtypeStruct((t, D_MODEL), F32),
        scratch_shapes=[pltpu.VMEM((D_MODEL, tm), F32), pltpu.VMEM((2, ec, tm), BF16)],
        compiler_params=pltpu.CompilerParams(
            dimension_semantics=("parallel", "arbitrary"), vmem_limit_bytes=VMEM_LIMIT_BYTES),
        name="peer",
    )(xn_t, u, v_t, rank2, p2, nsel, p1, x1, fnw)


def _tile(n, pref):
    while n % pref:
        pref //= 2
    return pref


def _trunk(x, w):
    bsz, seq, d = x.shape
    t = bsz * seq
    a, q, kf, gf, kb, gb, v, og, sga, sgb = _inproj(
        x.reshape(t, d), w["norm1"], w["w_in"], w["lbf"], w["lbb"], _tile(t, 512))
    r3 = lambda z: z.reshape(bsz, seq, z.shape[-1])
    o_f, o_b = _gla(r3(q), r3(kf), r3(gf), r3(kb), r3(gb), r3(v), _tile(seq, 128), _tile(bsz, GLA_GROUP))
    x1, xn_t, s_t = _merge(x, r3(a), o_f, o_b, r3(og), r3(sga), r3(sgb), w["merge"], _tile(seq, 256))
    rank2, p2, nsel, p1 = _select(s_t, _tile(t, 512))
    y = _peer(xn_t, w["u"], w["v_t"], rank2, p2, nsel, p1, x1.reshape(t, d), w["final_norm"],
              _tile(t, 512))
    return y.reshape(bsz, seq, d)


def kernel(x_prompt, x_sample, norm1_w, w_in, pool_w, pool_scale, lb_fwd, lb_bwd, hg_norm_w, w_pa, w_pb,
           w_out, norm2_w, peer_wq, peer_keys, peer_u, peer_v, final_norm_w):
    layer = 0
    lower_bound = lambda lb: jnp.cumsum(jax.nn.softmax(lb.astype(F32), axis=0), axis=0)[layer][None, :]
    row = lambda p: p.astype(F32).reshape(1, -1)
    w = {
        "norm1": row(norm1_w[layer]),
        "w_in": w_in[layer].astype(BF16),
        "lbf": lower_bound(lb_fwd),
        "lbb": lower_bound(lb_bwd),
        "merge": (pool_w[layer].astype(BF16), row(pool_scale[layer]), row(hg_norm_w[layer]),
                  w_pa[layer].astype(BF16), w_pb[layer].astype(BF16), w_out[layer].astype(BF16),
                  row(norm2_w[layer]), peer_wq[layer].astype(BF16),
                  peer_keys[layer].astype(BF16).reshape(2 * PEER_HEADS, N_KEYS, PEER_HALF)),
        "u": peer_u[layer].astype(BF16),
        "v_t": peer_v[layer].astype(BF16).reshape(-1, PEER_CHUNK, D_MODEL).transpose(0, 2, 1),
        "final_norm": row(final_norm_w),
    }
    return _trunk(x_prompt, w), _trunk(x_sample, w)
```

```python
import functools

import jax
import jax.numpy as jnp
from jax import lax
from jax.experimental import pallas as pl
from jax.experimental.pallas import tpu as pltpu

F32 = jnp.float32
BF16 = jnp.bfloat16

D_MODEL = 1024
POOL_WINDOWS = (2, 4, 8, 16)
POOL_GROUP_WIDTH = 128
POOL_WIDTH = 512
SUBLANES = 8
POOL_HALO = SUBLANES
HG_HEADS = 8
HG_HEAD_DIM = 128
PEER_HEADS = 8
PEER_HALF = 128
N_KEYS = 128
N_EXPERTS = N_KEYS * N_KEYS
PEER_TOPK = 16
NORM_EPS = 1e-6
NOT_RANKED = 100.0
PEER_PIECE = 256
PEER_CHUNK = 2048
GLA_GROUP = 4
GLA_CHUNK = 128
INPROJ_ROWS = 512
MERGE_ROWS = 512
PEER_TOKENS = 512
GELU_C0 = 0.7978845608028654
GELU_C1 = GELU_C0 * 0.044715
LOG2_E = 1.4426950408889634

VMEM_LIMIT_BYTES = 56 * 1024 * 1024


def _rms(x, w):
    return x * lax.rsqrt(jnp.mean(x * x, axis=-1, keepdims=True) + NORM_EPS) * w


def _dot(a, b):
    return jnp.dot(a, b, preferred_element_type=F32)


def _dot_nt(a, b):
    return lax.dot_general(a, b, (((1,), (1,)), ((), ())), preferred_element_type=F32)


def _dot_tn(a, b):
    return lax.dot_general(a, b, (((0,), (0,)), ((), ())), preferred_element_type=F32)


def _inproj_kernel(x_ref, n1_ref, w_ref, lbf_ref, lbb_ref,
                   a_ref, q_ref, kf_ref, gf_ref, kb_ref, gb_ref, v_ref, og_ref, sga_ref, sgb_ref):
    h = _rms(x_ref[...], n1_ref[...]).astype(BF16)

    def proj(block):
        c0 = POOL_WIDTH + (block - 1) * D_MODEL if block > 0 else 0
        width = D_MODEL if block > 0 else POOL_WIDTH
        return _dot(h, w_ref[:, c0:c0 + width])

    a_ref[...] = proj(0)
    qz = proj(1)
    q_ref[...] = (qz * jax.nn.sigmoid(qz)).astype(BF16)
    for block, lb_ref, k_ref, g_ref in ((2, lbf_ref, kf_ref, gf_ref), (3, lbb_ref, kb_ref, gb_ref)):
        lb = lb_ref[...]
        gap = (1.0 - lb) * jax.nn.sigmoid(proj(block))
        g_ref[...] = jnp.log(lb + gap)
        k_ref[...] = ((1.0 - lb) - gap).astype(BF16)
    v_ref[...] = proj(4).astype(BF16)
    ogz = proj(5)
    og_ref[...] = (ogz * jax.nn.sigmoid(ogz)).astype(BF16)
    sga_ref[...] = jax.nn.sigmoid(proj(6)).astype(BF16)
    sgb_ref[...] = jax.nn.sigmoid(proj(7)).astype(BF16)


def _inproj(x2, n1, w_in, lbf, lbb, tm):
    t = x2.shape[0]
    row = lambda w: pl.BlockSpec((tm, w), lambda i: (i, 0))
    full = lambda a: pl.BlockSpec(a.shape, lambda i: (0,) * a.ndim)
    wide = lambda dt: jax.ShapeDtypeStruct((t, D_MODEL), dt)
    out_shape = (jax.ShapeDtypeStruct((t, POOL_WIDTH), F32),
                 wide(BF16), wide(BF16), wide(F32), wide(BF16), wide(F32),
                 wide(BF16), wide(BF16), wide(BF16), wide(BF16))
    return pl.pallas_call(
        _inproj_kernel,
        grid=(t // tm,),
        in_specs=[row(D_MODEL), full(n1), full(w_in), full(lbf), full(lbb)],
        out_specs=(row(POOL_WIDTH),) + (row(D_MODEL),) * 9,
        out_shape=out_shape,
        compiler_params=pltpu.CompilerParams(
            dimension_semantics=("parallel",), vmem_limit_bytes=VMEM_LIMIT_BYTES),
        name="inproj",
    )(x2, n1, w_in, lbf, lbb)


def _split2(x):
    hi = x.astype(BF16)
    return hi, (x - hi.astype(F32)).astype(BF16)


def _gla_direction(q_ref, k_ref, g_ref, v_ref, o_ref, s_ref, reverse):
    c = q_ref.shape[0]
    hc = c // 2

    def travel_order(n):
        t_idx = lax.broadcasted_iota(jnp.int32, (n, n), 0)
        s_idx = lax.broadcasted_iota(jnp.int32, (n, n), 1)
        return (s_idx >= t_idx) if reverse else (s_idx <= t_idx)

    causal = travel_order(hc)
    tri = jnp.where(travel_order(c), 1.0, 0.0).astype(BF16)
    b_all = sum(_dot(tri, part) for part in _split2(g_ref[...] * LOG2_E))
    early, late = (slice(hc, c), slice(0, hc)) if reverse else (slice(0, hc), slice(hc, c))
    edge, end = (hc, 0) if reverse else (hc - 1, c - 1)
    mid_e, mid_l = early.start + hc // 2, late.start + hc // 2
    in_row_order = (lambda e, l: [l, e]) if reverse else (lambda e, l: [e, l])
    for h in range(HG_HEADS):
        sl = slice(h * HG_HEAD_DIM, (h + 1) * HG_HEAD_DIM)
        b = b_all[:, sl]
        q = q_ref[:, sl].astype(F32)
        k = k_ref[:, sl].astype(F32)
        v = v_ref[:, sl]
        row = lambda r: b[r:r + 1, :]
        q_ee = q[early] * jnp.exp2(b[early] - row(mid_e))
        k_ee = k[early] * jnp.exp2(row(mid_e) - b[early])
        q_ll = q[late] * jnp.exp2(b[late] - row(mid_l))
        k_ll = k[late] * jnp.exp2(row(mid_l) - b[late])
        q_le = q[late] * jnp.exp2(b[late] - row(edge))
        k_le = k[early] * jnp.exp2(row(edge) - b[early])

        def scores(qt, kt, masked):
            att = _dot_nt(qt.astype(BF16), kt.astype(BF16))
            return (jnp.where(causal, att, 0.0) if masked else att).astype(BF16)

        att_ee = scores(q_ee, k_ee, True)
        att_ll = scores(q_ll, k_ll, True)
        att_le = scores(q_le, k_le, False)
        st = s_ref[h]
        q_in = in_row_order(q_ee * jnp.exp2(row(mid_e)), q_le * jnp.exp2(row(edge)))
        k_out = in_row_order(k_le * jnp.exp2(row(end) - row(edge)), k_ll * jnp.exp2(row(end) - row(mid_l)))
        carried = _dot_nt(jnp.concatenate(q_in, axis=0).astype(BF16), st.astype(BF16))
        o_ref[early, sl] = (_dot(att_ee, v[early]) + carried[early]).astype(o_ref.dtype)
        o_ref[late, sl] = (_dot(att_le, v[early]) + _dot(att_ll, v[late]) + carried[late]).astype(o_ref.dtype)
        s_ref[h] = st * jnp.exp2(row(end)) + _dot_tn(v, jnp.concatenate(k_out, axis=0).astype(BF16))


def _gla_kernel(qf_ref, kf_ref, gf_ref, vf_ref, qb_ref, kb_ref, gb_ref, vb_ref,
                of_ref, ob_ref, sf_ref, sb_ref):
    @pl.when(pl.program_id(1) == 0)
    def _():
        sf_ref[...] = jnp.zeros_like(sf_ref)
        sb_ref[...] = jnp.zeros_like(sb_ref)

    for s in range(qf_ref.shape[0]):
        at = lambda *refs: [r.at[s] for r in refs]
        _gla_direction(*at(qf_ref, kf_ref, gf_ref, vf_ref, of_ref, sf_ref), reverse=False)
        _gla_direction(*at(qb_ref, kb_ref, gb_ref, vb_ref, ob_ref, sb_ref), reverse=True)


def _gla(q, kf, gf, kb, gb, v, chunk, group):
    bsz, seq, _ = q.shape
    n = seq // chunk
    fwd = pl.BlockSpec((group, chunk, D_MODEL), lambda b, i: (b, i, 0))
    bwd = pl.BlockSpec((group, chunk, D_MODEL), lambda b, i: (b, n - 1 - i, 0))
    state = pltpu.VMEM((group, HG_HEADS, HG_HEAD_DIM, HG_HEAD_DIM), F32)
    out = jax.ShapeDtypeStruct((bsz, seq, D_MODEL), BF16)
    return pl.pallas_call(
        _gla_kernel,
        grid=(bsz // group, n),
        in_specs=[fwd, fwd, fwd, fwd, bwd, bwd, bwd, bwd],
        out_specs=(fwd, bwd),
        out_shape=(out, out),
        scratch_shapes=[state, state],
        compiler_params=pltpu.CompilerParams(
            dimension_semantics=("parallel", "arbitrary"), vmem_limit_bytes=VMEM_LIMIT_BYTES),
        name="gla",
    )(q, kf, gf, v, q, kb, gb, v)


def _merge_kernel(seq, x_ref, ap_ref, a_ref, an_ref, of_ref, ob_ref, og_ref, sga_ref, sgb_ref,
                  pool_w_ref, pool_scale_ref, hgw_ref, w_pa_ref, w_pb_ref, w_out_ref,
                  n2_ref, wq_ref, keys_ref,
                  x1_ref, xnt_ref, st_ref):
    tm = x_ref.shape[0]
    ext = tm + 2 * POOL_HALO
    a_ext = jnp.concatenate([ap_ref[...], a_ref[...], an_ref[...]], axis=0)
    a_hi = a_ext.astype(BF16)
    a_lo = (a_ext - a_hi.astype(F32)).astype(BF16)
    t0 = pl.program_id(1) * tm
    t_row = t0 + lax.broadcasted_iota(jnp.int32, (tm, ext), 0)
    t_col = t0 - POOL_HALO + lax.broadcasted_iota(jnp.int32, (tm, ext), 1)
    t_out = t0 + lax.broadcasted_iota(jnp.int32, (tm, POOL_GROUP_WIDTH), 0)
    a_cur = a_ref[...]
    pa_parts = []
    for g, win in enumerate(POOL_WINDOWS):
        sl = slice(g * POOL_GROUP_WIDTH, (g + 1) * POOL_GROUP_WIDTH)
        lo = jnp.maximum(t_row - win // 2, 0)
        hi = jnp.minimum(t_row + win // 2, seq)
        band = jnp.where((t_col >= lo) & (t_col < hi), 1.0, 0.0).astype(BF16)
        wsum = _dot(band, a_hi[:, sl]) + _dot(band, a_lo[:, sl])
        cnt = (jnp.minimum(t_out + win // 2, seq) - jnp.maximum(t_out - win // 2, 0)).astype(F32)
        pooled = wsum / cnt - a_cur[:, sl]
        pa_parts.append(_dot(pooled.astype(BF16), pool_w_ref[g]))
    pa = jnp.concatenate(pa_parts, axis=-1) * pool_scale_ref[...]
    pa = _dot(pa.astype(BF16), w_pa_ref[...])

    o = of_ref[...].astype(F32) + ob_ref[...].astype(F32)
    o_parts = []
    for h in range(HG_HEADS):
        oh = o[:, h * HG_HEAD_DIM:(h + 1) * HG_HEAD_DIM]
        o_parts.append(oh * lax.rsqrt(jnp.mean(oh * oh, axis=-1, keepdims=True) + NORM_EPS))
    o = jnp.concatenate(o_parts, axis=-1) * hgw_ref[...] * og_ref[...].astype(F32)
    pb = _dot(o.astype(BF16), w_pb_ref[...])

    merged = sga_ref[...].astype(F32) * pa + sgb_ref[...].astype(F32) * pb
    x1 = x_ref[...] + _dot(merged.astype(BF16), w_out_ref[...])
    x1_ref[...] = x1

    xn = _rms(x1, n2_ref[...])
    xnt_ref[...] = xn.T.astype(BF16)
    qp = _dot(xn.astype(BF16), wq_ref[...]).astype(BF16)
    for j in range(2 * PEER_HEADS):
        st_ref[j] = _dot_nt(keys_ref[j], qp[:, j * PEER_HALF:(j + 1) * PEER_HALF])


def _merge(x, a, o_f, o_b, og, sga, sgb, wts, tm):
    bsz, seq, _ = x.shape
    nt = seq // tm
    t = bsz * seq
    hb = tm // POOL_HALO
    last_halo = seq // POOL_HALO - 1
    row = lambda w: pl.BlockSpec((None, tm, w), lambda b, i: (b, i, 0))
    prev = pl.BlockSpec((None, POOL_HALO, POOL_WIDTH), lambda b, i: (b, jnp.maximum(i * hb - 1, 0), 0))
    nxt = pl.BlockSpec((None, POOL_HALO, POOL_WIDTH), lambda b, i: (b, jnp.minimum((i + 1) * hb, last_halo), 0))
    full = lambda w: pl.BlockSpec(w.shape, lambda b, i: (0,) * w.ndim)
    return pl.pallas_call(
        functools.partial(_merge_kernel, seq),
        grid=(bsz, nt),
        in_specs=[row(D_MODEL), prev, row(POOL_WIDTH), nxt] + [row(D_MODEL)] * 5 + [full(w) for w in wts],
        out_specs=(row(D_MODEL),
                   pl.BlockSpec((D_MODEL, tm), lambda b, i: (0, b * nt + i)),
                   pl.BlockSpec((2 * PEER_HEADS, N_KEYS, tm), lambda b, i: (0, 0, b * nt + i))),
        out_shape=(jax.ShapeDtypeStruct((bsz, seq, D_MODEL), F32),
                   jax.ShapeDtypeStruct((D_MODEL, t), BF16),
                   jax.ShapeDtypeStruct((2 * PEER_HEADS, N_KEYS, t), F32)),
        compiler_params=pltpu.CompilerParams(
            dimension_semantics=("parallel", "parallel"), vmem_limit_bytes=VMEM_LIMIT_BYTES),
        name="merge",
    )(x, a, a, a, o_f, o_b, og, sga, sgb, *wts)


def _sort_network(n):
    def merge(lo, hi, r):
        step = r * 2
        if step < hi - lo:
            yield from merge(lo, hi, step)
            yield from merge(lo + r, hi, step)
            yield from ((i, i + r) for i in range(lo + r, hi - r, step))
        else:
            yield (lo, lo + r)

    def sort(lo, hi):
        if hi > lo:
            mid = lo + (hi - lo) // 2
            yield from sort(lo, mid)
            yield from sort(mid + 1, hi)
            yield from merge(lo, hi, 1)

    return tuple(sort(0, n - 1))


def _exchange(v, i, j):
    if v[i] is None:
        v[i], v[j] = v[j], None
    elif v[j] is not None:
        v[i], v[j] = jnp.maximum(v[i], v[j]), jnp.minimum(v[i], v[j])


def _top_sorted(pieces):
    k = PEER_TOPK
    v = list(pieces) + [None] * (k - len(pieces))
    for i, j in _sort_network(k):
        _exchange(v, i, j)
    both = lambda a, b: b if a is None else a if b is None else jnp.maximum(a, b)
    shift = SUBLANES // 2
    while shift:
        w = [None if x is None else pltpu.roll(x, shift, axis=0) for x in v]
        v = [both(v[d], w[k - 1 - d]) for d in range(k)]
        stride = k // 2
        while stride:
            for i in range(k):
                if not i & stride:
                    _exchange(v, i, i + stride)
            stride //= 2
        shift //= 2
    return v


def _sublane_sum(x):
    shift = SUBLANES // 2
    while shift:
        x = x + pltpu.roll(x, shift, axis=0)
        shift //= 2
    return x


def _select_kernel(s_ref, rank2_ref, p2_ref, nsel_ref, p1_ref):
    k, half = PEER_TOPK, PEER_TOPK // 2
    assert N_KEYS == k * SUBLANES and half == SUBLANES
    s1 = [s_ref[0, d * SUBLANES:(d + 1) * SUBLANES, :] for d in range(k)]
    s2 = [s_ref[1, d * SUBLANES:(d + 1) * SUBLANES, :] for d in range(k)]
    top1 = _top_sorted(s1)
    top2 = _top_sorted(s2)
    row = lax.broadcasted_iota(jnp.int32, s1[0].shape, 0)
    a_stack = [functools.reduce(lambda acc, c: jnp.where(row == c, top1[o + c], acc), range(half), top1[o])
               for o in (0, half)]
    b_stack = [functools.reduce(lambda acc, c: jnp.where(row == c, top2[o + c], acc), range(half), top2[o])
               for o in (0, half)]
    cands = [top1[0] + b_stack[0], top1[0] + b_stack[1]]
    cands += [top1[r] + b_stack[0] for r in range(1, half)]
    cands += [a_stack[1] + top2[0]]
    thr = _top_sorted(cands)[k - 1]
    best = top1[0] + top2[0]
    chosen = [x >= thr for x in cands]
    z = _sublane_sum(sum(jnp.where(ch, jnp.exp(x - best), 0.0) for ch, x in zip(chosen, cands)))
    count = [jnp.where(ch, 1.0, 0.0) for ch in chosen]
    n_sel = [_sublane_sum(count[0] + count[1])] + [_sublane_sum(count[r + 1]) for r in range(1, half)]
    last = count[half + 1]
    n_sel += [_sublane_sum(jnp.where(row == c, last, 0.0)) for c in range(half)]
    scale = 0.5 / z
    rank2, p2, nsel_out, p1_out = [], [], [], []
    for d in range(k):
        nsel = jnp.zeros_like(s1[d])
        rank = jnp.full_like(s2[d], NOT_RANKED)
        for r in reversed(range(k)):
            nsel = jnp.where(s1[d] == top1[r], n_sel[r], nsel)
            rank = jnp.where(s2[d] == top2[r], float(r + 1), rank)
        nsel_out.append(nsel)
        p1_out.append(jnp.exp(s1[d] - top1[0]))
        rank2.append(rank)
        p2.append(jnp.exp(s2[d] - top2[0]) * scale)
    for ref, parts in ((rank2_ref, rank2), (p2_ref, p2), (nsel_ref, nsel_out), (p1_ref, p1_out)):
        ref[...] = jnp.concatenate(parts, axis=0).astype(ref.dtype)


def _select(s_t, tl):
    t = s_t.shape[-1]
    s4 = s_t.reshape(PEER_HEADS, 2, N_KEYS, t)
    out = lambda dt: jax.ShapeDtypeStruct((PEER_HEADS, N_KEYS, t), dt)
    spec = pl.BlockSpec((None, N_KEYS, tl), lambda i, h: (h, 0, i))
    return pl.pallas_call(
        _select_kernel,
        grid=(t // tl, PEER_HEADS),
        in_specs=[pl.BlockSpec((None, 2, N_KEYS, tl), lambda i, h: (h, 0, 0, i))],
        out_specs=(spec,) * 4,
        out_shape=(out(BF16), out(BF16), out(F32), out(F32)),
        compiler_params=pltpu.CompilerParams(
            dimension_semantics=("parallel", "parallel"), vmem_limit_bytes=VMEM_LIMIT_BYTES),
        name="select",
    )(s4)


def _peer_gates(act, rank2_ref, p2_ref, nsel_ref, p1_ref, gate_ref, slot, cols):
    tp = act.shape[1]
    for il in range(act.shape[0] // N_KEYS):
        x = act[il * N_KEYS:(il + 1) * N_KEYS, :].astype(BF16)
        w = None
        for h in range(PEER_HEADS):
            ns = jnp.broadcast_to(nsel_ref[h, il:il + 1, cols], (N_KEYS, tp)).astype(BF16)
            p1 = jnp.broadcast_to(p1_ref[h, il:il + 1, cols], (N_KEYS, tp)).astype(BF16)
            term = jnp.where(rank2_ref[h, :, cols] <= ns, p2_ref[h, :, cols], 0) * p1
            w = term if w is None else w + term
        inner = x * (GELU_C0 + GELU_C1 * (x * x))
        gate_ref[slot, il * N_KEYS:(il + 1) * N_KEYS, cols] = (x + x * jnp.tanh(inner)) * w


def _peer_kernel(xnt_ref, u_ref, vt_ref, rank2_ref, p2_ref, nsel_ref, p1_ref, x1_ref, fnw_ref,
                 y_ref, acc_ref, gate_ref):
    c = pl.program_id(1)
    last = pl.num_programs(1) - 1
    slot = lax.rem(c, 2)
    tm = xnt_ref.shape[1]
    pieces = [slice(p * PEER_PIECE, (p + 1) * PEER_PIECE) for p in range(tm // PEER_PIECE)]

    def pre_activations():
        return [_dot(u_ref[...], xnt_ref[:, cols]) for cols in pieces]

    def v_product(prev):
        return [_dot(vt_ref[...], gate_ref[prev, :, cols]) for cols in pieces]

    def build(acts, slot):
        for act, cols in zip(acts, pieces):
            _peer_gates(act, rank2_ref, p2_ref, nsel_ref, p1_ref, gate_ref, slot, cols)

    @pl.when(c == 0)
    def _():
        acc_ref[...] = jnp.zeros_like(acc_ref)
        build(pre_activations(), 0)

    @pl.when((c > 0) & (c < last))
    def _():
        acts = pre_activations()
        for upd, cols in zip(v_product(1 - slot), pieces):
            acc_ref[:, cols] += upd
        build(acts, slot)

    @pl.when(c == last)
    def _():
        out = acc_ref[...] + jnp.concatenate(v_product(1 - slot), axis=1)
        y_ref[...] = _rms(x1_ref[...] + out.T, fnw_ref[...])


def _peer(xn_t, u, v_t, rank2, p2, nsel, p1, x1, fnw, tm):
    t = xn_t.shape[1]
    nch, _, ec = v_t.shape
    ni = ec // N_KEYS
    chunk = lambda c, lag: jnp.clip(c - lag, 0, nch - 1)
    tok3 = pl.BlockSpec((PEER_HEADS, N_KEYS, tm), lambda i, c: (0, 0, i))
    chunk3 = pl.BlockSpec((PEER_HEADS, ni, tm), lambda i, c: (0, chunk(c, 0), i))
    return pl.pallas_call(
        _peer_kernel,
        grid=(t // tm, nch + 1),
        in_specs=[pl.BlockSpec((D_MODEL, tm), lambda i, c: (0, i)),
                  pl.BlockSpec((ec, D_MODEL), lambda i, c: (chunk(c, 0), 0)),
                  pl.BlockSpec((None, D_MODEL, ec), lambda i, c: (chunk(c, 1), 0, 0)),
                  tok3, tok3, chunk3, chunk3,
                  pl.BlockSpec((tm, D_MODEL), lambda i, c: (i, 0)),
                  pl.BlockSpec((1, D_MODEL), lambda i, c: (0, 0))],
        out_specs=pl.BlockSpec((tm, D_MODEL), lambda i, c: (i, 0)),
        out_shape=jax.ShapeDtypeStruct((t, D_MODEL), F32),
        scratch_shapes=[pltpu.VMEM((D_MODEL, tm), F32), pltpu.VMEM((2, ec, tm), BF16)],
        compiler_params=pltpu.CompilerParams(
            dimension_semantics=("parallel", "arbitrary"), vmem_limit_bytes=VMEM_LIMIT_BYTES),
        name="peer",
    )(xn_t, u, v_t, rank2, p2, nsel, p1, x1, fnw)


def _fit(n, pref):
    while n % pref:
        pref //= 2
    return pref


def _trunk(x, w):
    bsz, seq, d = x.shape
    assert d == D_MODEL and seq % (2 * POOL_HALO) == 0
    t = bsz * seq
    a, q, kf, gf, kb, gb, v, og, sga, sgb = _inproj(
        x.reshape(t, d), w["norm1"], w["w_in"], w["lbf"], w["lbb"], _fit(t, INPROJ_ROWS))
    r3 = lambda z: z.reshape(bsz, seq, z.shape[-1])
    o_f, o_b = _gla(r3(q), r3(kf), r3(gf), r3(kb), r3(gb), r3(v), _fit(seq, GLA_CHUNK), _fit(bsz, GLA_GROUP))
    x1, xn_t, s_t = _merge(x, r3(a), o_f, o_b, r3(og), r3(sga), r3(sgb), w["merge"], _fit(seq, MERGE_ROWS))
    rank2, p2, nsel, p1 = _select(s_t, _fit(t, PEER_TOKENS))
    y = _peer(xn_t, w["u"], w["v_t"], rank2, p2, nsel, p1, x1.reshape(t, d), w["final_norm"],
              _fit(t, PEER_TOKENS))
    return y.reshape(bsz, seq, d)


def kernel(x_prompt, x_sample, norm1_w, w_in, pool_w, pool_scale, lb_fwd, lb_bwd, hg_norm_w, w_pa, w_pb,
           w_out, norm2_w, peer_wq, peer_keys, peer_u, peer_v, final_norm_w):
    layer = 0
    lower_bound = lambda lb: jnp.cumsum(jax.nn.softmax(lb.astype(F32), axis=0), axis=0)[layer][None, :]
    row = lambda p: p.astype(F32).reshape(1, -1)
    w = {
        "norm1": row(norm1_w[layer]),
        "w_in": w_in[layer].astype(BF16),
        "lbf": lower_bound(lb_fwd),
        "lbb": lower_bound(lb_bwd),
        "merge": (pool_w[layer].astype(BF16), row(pool_scale[layer]), row(hg_norm_w[layer]),
                  w_pa[layer].astype(BF16), w_pb[layer].astype(BF16), w_out[layer].astype(BF16),
                  row(norm2_w[layer]), peer_wq[layer].astype(BF16),
                  peer_keys[layer].astype(BF16).reshape(2 * PEER_HEADS, N_KEYS, PEER_HALF)),
        "u": peer_u[layer].astype(BF16),
        "v_t": peer_v[layer].astype(BF16).reshape(-1, PEER_CHUNK, D_MODEL).transpose(0, 2, 1),
        "final_norm": row(final_norm_w),
    }
    return _trunk(x_prompt, w), _trunk(x_sample, w)
```

```python
import functools

import jax
import jax.numpy as jnp
from jax import lax
from jax.experimental import pallas as pl
from jax.experimental.pallas import tpu as pltpu

F32 = jnp.float32
BF16 = jnp.bfloat16

D_MODEL = 1024
POOL_WINDOWS = (2, 4, 8, 16)
POOL_GROUP_WIDTH = 128
POOL_WIDTH = 512
SUBLANES = 8
POOL_HALO = SUBLANES
HG_HEADS = 8
HG_HEAD_DIM = 128
PEER_HEADS = 8
PEER_HALF = 128
N_KEYS = 128
N_EXPERTS = N_KEYS * N_KEYS
PEER_TOPK = 16
NORM_EPS = 1e-6
NOT_RANKED = 100.0
PEER_PIECE = 256
PEER_CHUNK = 2048
GLA_GROUP = 4
GLA_CHUNK = 128
INPROJ_ROWS = 512
MERGE_ROWS = 512
PEER_TOKENS = 512
GELU_C0 = 0.7978845608028654
GELU_C1 = GELU_C0 * 0.044715
LOG2_E = 1.4426950408889634

VMEM_LIMIT_BYTES = 56 * 1024 * 1024


def _rms(x, w):
    return x * lax.rsqrt(jnp.mean(x * x, axis=-1, keepdims=True) + NORM_EPS) * w


def _dot(a, b):
    return jnp.dot(a, b, preferred_element_type=F32)


def _dot_nt(a, b):
    return lax.dot_general(a, b, (((1,), (1,)), ((), ())), preferred_element_type=F32)


def _dot_tn(a, b):
    return lax.dot_general(a, b, (((0,), (0,)), ((), ())), preferred_element_type=F32)


def _inproj_kernel(x_ref, n1_ref, w_ref, lbf_ref, lbb_ref,
                   a_ref, q_ref, kf_ref, gf_ref, kb_ref, gb_ref, v_ref, og_ref, sga_ref, sgb_ref):
    h = _rms(x_ref[...], n1_ref[...]).astype(BF16)

    def proj(block):
        c0 = POOL_WIDTH + (block - 1) * D_MODEL if block > 0 else 0
        width = D_MODEL if block > 0 else POOL_WIDTH
        return _dot(h, w_ref[:, c0:c0 + width])

    a_ref[...] = proj(0)
    qz = proj(1)
    q_ref[...] = (qz * jax.nn.sigmoid(qz)).astype(BF16)
    for block, lb_ref, k_ref, g_ref in ((2, lbf_ref, kf_ref, gf_ref), (3, lbb_ref, kb_ref, gb_ref)):
        lb = lb_ref[...]
        gap = (1.0 - lb) * jax.nn.sigmoid(proj(block))
        g_ref[...] = jnp.log(lb + gap)
        k_ref[...] = ((1.0 - lb) - gap).astype(BF16)
    v_ref[...] = proj(4).astype(BF16)
    ogz = proj(5)
    og_ref[...] = (ogz * jax.nn.sigmoid(ogz)).astype(BF16)
    sga_ref[...] = jax.nn.sigmoid(proj(6)).astype(BF16)
    sgb_ref[...] = jax.nn.sigmoid(proj(7)).astype(BF16)


def _inproj(x2, n1, w_in, lbf, lbb, tm):
    t = x2.shape[0]
    row = lambda w: pl.BlockSpec((tm, w), lambda i: (i, 0))
    full = lambda a: pl.BlockSpec(a.shape, lambda i: (0,) * a.ndim)
    wide = lambda dt: jax.ShapeDtypeStruct((t, D_MODEL), dt)
    out_shape = (jax.ShapeDtypeStruct((t, POOL_WIDTH), F32),
                 wide(BF16), wide(BF16), wide(F32), wide(BF16), wide(F32),
                 wide(BF16), wide(BF16), wide(BF16), wide(BF16))
    return pl.pallas_call(
        _inproj_kernel,
        grid=(t // tm,),
        in_specs=[row(D_MODEL), full(n1), full(w_in), full(lbf), full(lbb)],
        out_specs=(row(POOL_WIDTH),) + (row(D_MODEL),) * 9,
        out_shape=out_shape,
        compiler_params=pltpu.CompilerParams(
            dimension_semantics=("parallel",), vmem_limit_bytes=VMEM_LIMIT_BYTES),
        name="inproj",
    )(x2, n1, w_in, lbf, lbb)


def _split2(x):
    hi = x.astype(BF16)
    return hi, (x - hi.astype(F32)).astype(BF16)


def _gla_direction(q_ref, k_ref, g_ref, v_ref, o_ref, s_ref, reverse):
    c = q_ref.shape[0]
    hc = c // 2

    def travel_order(n):
        t_idx = lax.broadcasted_iota(jnp.int32, (n, n), 0)
        s_idx = lax.broadcasted_iota(jnp.int32, (n, n), 1)
        return (s_idx >= t_idx) if reverse else (s_idx <= t_idx)

    causal = travel_order(hc)
    tri = jnp.where(travel_order(c), 1.0, 0.0).astype(BF16)
    b_all = sum(_dot(tri, part) for part in _split2(g_ref[...] * LOG2_E))
    early, late = (slice(hc, c), slice(0, hc)) if reverse else (slice(0, hc), slice(hc, c))
    edge, end = (hc, 0) if reverse else (hc - 1, c - 1)
    mid_e, mid_l = early.start + hc // 2, late.start + hc // 2
    in_row_order = (lambda e, l: [l, e]) if reverse else (lambda e, l: [e, l])
    for h in range(HG_HEADS):
        sl = slice(h * HG_HEAD_DIM, (h + 1) * HG_HEAD_DIM)
        b = b_all[:, sl]
        q = q_ref[:, sl].astype(F32)
        k = k_ref[:, sl].astype(F32)
        v = v_ref[:, sl]
        row = lambda r: b[r:r + 1, :]
        q_ee = q[early] * jnp.exp2(b[early] - row(mid_e))
        k_ee = k[early] * jnp.exp2(row(mid_e) - b[early])
        q_ll = q[late] * jnp.exp2(b[late] - row(mid_l))
        k_ll = k[late] * jnp.exp2(row(mid_l) - b[late])
        q_le = q[late] * jnp.exp2(b[late] - row(edge))
        k_le = k[early] * jnp.exp2(row(edge) - b[early])

        def scores(qt, kt, masked):
            att = _dot_nt(qt.astype(BF16), kt.astype(BF16))
            return (jnp.where(causal, att, 0.0) if masked else att).astype(BF16)

        att_ee = scores(q_ee, k_ee, True)
        att_ll = scores(q_ll, k_ll, True)
        att_le = scores(q_le, k_le, False)
        st = s_ref[h]
        q_in = in_row_order(q_ee * jnp.exp2(row(mid_e)), q_le * jnp.exp2(row(edge)))
        k_out = in_row_order(k_le * jnp.exp2(row(end) - row(edge)), k_ll * jnp.exp2(row(end) - row(mid_l)))
        carried = _dot_nt(jnp.concatenate(q_in, axis=0).astype(BF16), st.astype(BF16))
        o_ref[early, sl] = (_dot(att_ee, v[early]) + carried[early]).astype(o_ref.dtype)
        o_ref[late, sl] = (_dot(att_le, v[early]) + _dot(att_ll, v[late]) + carried[late]).astype(o_ref.dtype)
        s_ref[h] = st * jnp.exp2(row(end)) + _dot_tn(v, jnp.concatenate(k_out, axis=0).astype(BF16))


def _gla_kernel(qf_ref, kf_ref, gf_ref, vf_ref, qb_ref, kb_ref, gb_ref, vb_ref,
                of_ref, ob_ref, sf_ref, sb_ref):
    @pl.when(pl.program_id(1) == 0)
    def _():
        sf_ref[...] = jnp.zeros_like(sf_ref)
        sb_ref[...] = jnp.zeros_like(sb_ref)

    for s in range(qf_ref.shape[0]):
        at = lambda *refs: [r.at[s] for r in refs]
        _gla_direction(*at(qf_ref, kf_ref, gf_ref, vf_ref, of_ref, sf_ref), reverse=False)
        _gla_direction(*at(qb_ref, kb_ref, gb_ref, vb_ref, ob_ref, sb_ref), reverse=True)


def _gla(q, kf, gf, kb, gb, v, chunk, group):
    bsz, seq, _ = q.shape
    n = seq // chunk
    fwd = pl.BlockSpec((group, chunk, D_MODEL), lambda b, i: (b, i, 0))
    bwd = pl.BlockSpec((group, chunk, D_MODEL), lambda b, i: (b, n - 1 - i, 0))
    state = pltpu.VMEM((group, HG_HEADS, HG_HEAD_DIM, HG_HEAD_DIM), F32)
    out = jax.ShapeDtypeStruct((bsz, seq, D_MODEL), BF16)
    return pl.pallas_call(
        _gla_kernel,
        grid=(bsz // group, n),
        in_specs=[fwd, fwd, fwd, fwd, bwd, bwd, bwd, bwd],
        out_specs=(fwd, bwd),
        out_shape=(out, out),
        scratch_shapes=[state, state],
        compiler_params=pltpu.CompilerParams(
            dimension_semantics=("parallel", "arbitrary"), vmem_limit_bytes=VMEM_LIMIT_BYTES),
        name="gla",
    )(q, kf, gf, v, q, kb, gb, v)


def _merge_kernel(seq, x_ref, ap_ref, a_ref, an_ref, of_ref, ob_ref, og_ref, sga_ref, sgb_ref,
                  pool_w_ref, pool_scale_ref, hgw_ref, w_pa_ref, w_pb_ref, w_out_ref,
                  n2_ref, wq_ref, keys_ref,
                  x1_ref, xnt_ref, st_ref):
    tm = x_ref.shape[0]
    ext = tm + 2 * POOL_HALO
    a_ext = jnp.concatenate([ap_ref[...], a_ref[...], an_ref[...]], axis=0)
    a_hi = a_ext.astype(BF16)
    a_lo = (a_ext - a_hi.astype(F32)).astype(BF16)
    t0 = pl.program_id(1) * tm
    t_row = t0 + lax.broadcasted_iota(jnp.int32, (tm, ext), 0)
    t_col = t0 - POOL_HALO + lax.broadcasted_iota(jnp.int32, (tm, ext), 1)
    t_out = t0 + lax.broadcasted_iota(jnp.int32, (tm, POOL_GROUP_WIDTH), 0)
    a_cur = a_ref[...]
    pa_parts = []
    for g, win in enumerate(POOL_WINDOWS):
        sl = slice(g * POOL_GROUP_WIDTH, (g + 1) * POOL_GROUP_WIDTH)
        lo = jnp.maximum(t_row - win // 2, 0)
        hi = jnp.minimum(t_row + win // 2, seq)
        band = jnp.where((t_col >= lo) & (t_col < hi), 1.0, 0.0).astype(BF16)
        wsum = _dot(band, a_hi[:, sl]) + _dot(band, a_lo[:, sl])
        cnt = (jnp.minimum(t_out + win // 2, seq) - jnp.maximum(t_out - win // 2, 0)).astype(F32)
        pooled = wsum / cnt - a_cur[:, sl]
        pa_parts.append(_dot(pooled.astype(BF16), pool_w_ref[g]))
    pa = jnp.concatenate(pa_parts, axis=-1) * pool_scale_ref[...]
    pa = _dot(pa.astype(BF16), w_pa_ref[...])

    o = of_ref[...].astype(F32) + ob_ref[...].astype(F32)
    o_parts = []
    for h in range(HG_HEADS):
        oh = o[:, h * HG_HEAD_DIM:(h + 1) * HG_HEAD_DIM]
        o_parts.append(oh * lax.rsqrt(jnp.mean(oh * oh, axis=-1, keepdims=True) + NORM_EPS))
    o = jnp.concatenate(o_parts, axis=-1) * hgw_ref[...] * og_ref[...].astype(F32)
    pb = _dot(o.astype(BF16), w_pb_ref[...])

    merged = sga_ref[...].astype(F32) * pa + sgb_ref[...].astype(F32) * pb
    x1 = x_ref[...] + _dot(merged.astype(BF16), w_out_ref[...])
    x1_ref[...] = x1

    xn = _rms(x1, n2_ref[...])
    xnt_ref[...] = xn.T.astype(BF16)
    qp = _dot(xn.astype(BF16), wq_ref[...]).astype(BF16)
    for j in range(2 * PEER_HEADS):
        st_ref[j] = _dot_nt(keys_ref[j], qp[:, j * PEER_HALF:(j + 1) * PEER_HALF])


def _merge(x, a, o_f, o_b, og, sga, sgb, wts, tm):
    bsz, seq, _ = x.shape
    nt = seq // tm
    t = bsz * seq
    hb = tm // POOL_HALO
    last_halo = seq // POOL_HALO - 1
    row = lambda w: pl.BlockSpec((None, tm, w), lambda b, i: (b, i, 0))
    prev = pl.BlockSpec((None, POOL_HALO, POOL_WIDTH), lambda b, i: (b, jnp.maximum(i * hb - 1, 0), 0))
    nxt = pl.BlockSpec((None, POOL_HALO, POOL_WIDTH), lambda b, i: (b, jnp.minimum((i + 1) * hb, last_halo), 0))
    full = lambda w: pl.BlockSpec(w.shape, lambda b, i: (0,) * w.ndim)
    return pl.pallas_call(
        functools.partial(_merge_kernel, seq),
        grid=(bsz, nt),
        in_specs=[row(D_MODEL), prev, row(POOL_WIDTH), nxt] + [row(D_MODEL)] * 5 + [full(w) for w in wts],
        out_specs=(row(D_MODEL),
                   pl.BlockSpec((D_MODEL, tm), lambda b, i: (0, b * nt + i)),
                   pl.BlockSpec((2 * PEER_HEADS, N_KEYS, tm), lambda b, i: (0, 0, b * nt + i))),
        out_shape=(jax.ShapeDtypeStruct((bsz, seq, D_MODEL), F32),
                   jax.ShapeDtypeStruct((D_MODEL, t), BF16),
                   jax.ShapeDtypeStruct((2 * PEER_HEADS, N_KEYS, t), F32)),
        compiler_params=pltpu.CompilerParams(
            dimension_semantics=("parallel", "parallel"), vmem_limit_bytes=VMEM_LIMIT_BYTES),
        name="merge",
    )(x, a, a, a, o_f, o_b, og, sga, sgb, *wts)


def _sort_network(n):
    def merge(lo, hi, r):
        step = r * 2
        if step < hi - lo:
            yield from merge(lo, hi, step)
            yield from merge(lo + r, hi, step)
            yield from ((i, i + r) for i in range(lo + r, hi - r, step))
        else:
            yield (lo, lo + r)

    def sort(lo, hi):
        if hi > lo:
            mid = lo + (hi - lo) // 2
            yield from sort(lo, mid)
            yield from sort(mid + 1, hi)
            yield from merge(lo, hi, 1)

    return tuple(sort(0, n - 1))


def _exchange(v, i, j):
    if v[i] is None:
        v[i], v[j] = v[j], None
    elif v[j] is not None:
        v[i], v[j] = jnp.maximum(v[i], v[j]), jnp.minimum(v[i], v[j])


def _top_sorted(pieces):
    k = PEER_TOPK
    v = list(pieces) + [None] * (k - len(pieces))
    for i, j in _sort_network(k):
        _exchange(v, i, j)
    both = lambda a, b: b if a is None else a if b is None else jnp.maximum(a, b)
    shift = SUBLANES // 2
    while shift:
        w = [None if x is None else pltpu.roll(x, shift, axis=0) for x in v]
        v = [both(v[d], w[k - 1 - d]) for d in range(k)]
        stride = k // 2
        while stride:
            for i in range(k):
                if not i & stride:
                    _exchange(v, i, i + stride)
            stride //= 2
        shift //= 2
    return v


def _rank_among(x, tops):
    n = len(tops)
    rank = 1.0
    masks = []
    size = n
    while size > 1:
        half = size // 2
        pivots = [tops[start + half - 1] for start in range(0, n, size)]
        for m in reversed(masks):
            pivots = [jnp.where(m, hi, lo) for lo, hi in zip(pivots[0::2], pivots[1::2])]
        above = pivots[0] > x
        rank = rank + jnp.where(above, float(half), 0.0)
        masks.append(above)
        size = half
    return rank


def _sublane_sum(x):
    shift = SUBLANES // 2
    while shift:
        x = x + pltpu.roll(x, shift, axis=0)
        shift //= 2
    return x


def _select_kernel(s_ref, rank2_ref, p2_ref, nsel_ref, p1_ref):
    k, half = PEER_TOPK, PEER_TOPK // 2
    assert N_KEYS == k * SUBLANES and half == SUBLANES
    s1 = [s_ref[0, d * SUBLANES:(d + 1) * SUBLANES, :] for d in range(k)]
    s2 = [s_ref[1, d * SUBLANES:(d + 1) * SUBLANES, :] for d in range(k)]
    top1 = _top_sorted(s1)
    top2 = _top_sorted(s2)
    row = lax.broadcasted_iota(jnp.int32, s1[0].shape, 0)
    a_stack = [functools.reduce(lambda acc, c: jnp.where(row == c, top1[o + c], acc), range(half), top1[o])
               for o in (0, half)]
    b_stack = [functools.reduce(lambda acc, c: jnp.where(row == c, top2[o + c], acc), range(half), top2[o])
               for o in (0, half)]
    cands = [top1[0] + b_stack[0], top1[0] + b_stack[1]]
    cands += [top1[r] + b_stack[0] for r in range(1, half)]
    cands += [a_stack[1] + top2[0]]
    thr = _top_sorted(cands)[k - 1]
    best = top1[0] + top2[0]
    chosen = [x >= thr for x in cands]
    z = _sublane_sum(sum(jnp.where(ch, jnp.exp(x - best), 0.0) for ch, x in zip(chosen, cands)))
    count = [jnp.where(ch, 1.0, 0.0) for ch in chosen]
    n_sel = [_sublane_sum(count[0] + count[1])] + [_sublane_sum(count[r + 1]) for r in range(1, half)]
    last = count[half + 1]
    n_sel += [_sublane_sum(jnp.where(row == c, last, 0.0)) for c in range(half)]
    scale = 0.5 / z
    rank2, p2, nsel_out, p1_out = [], [], [], []
    for d in range(k):
        nsel = jnp.zeros_like(s1[d])
        for r in reversed(range(k)):
            nsel = jnp.where(s1[d] == top1[r], n_sel[r], nsel)
        rank = jnp.where(s2[d] >= top2[k - 1], _rank_among(s2[d], top2), NOT_RANKED)
        nsel_out.append(nsel)
        p1_out.append(jnp.exp(s1[d] - top1[0]))
        rank2.append(rank)
        p2.append(jnp.exp(s2[d] - top2[0]) * scale)
    for ref, parts in ((rank2_ref, rank2), (p2_ref, p2), (nsel_ref, nsel_out), (p1_ref, p1_out)):
        ref[...] = jnp.concatenate(parts, axis=0).astype(ref.dtype)


def _select(s_t, tl):
    t = s_t.shape[-1]
    s4 = s_t.reshape(PEER_HEADS, 2, N_KEYS, t)
    out = lambda dt: jax.ShapeDtypeStruct((PEER_HEADS, N_KEYS, t), dt)
    spec = pl.BlockSpec((None, N_KEYS, tl), lambda i, h: (h, 0, i))
    return pl.pallas_call(
        _select_kernel,
        grid=(t // tl, PEER_HEADS),
        in_specs=[pl.BlockSpec((None, 2, N_KEYS, tl), lambda i, h: (h, 0, 0, i))],
        out_specs=(spec,) * 4,
        out_shape=(out(BF16), out(BF16), out(F32), out(F32)),
        compiler_params=pltpu.CompilerParams(
            dimension_semantics=("parallel", "parallel"), vmem_limit_bytes=VMEM_LIMIT_BYTES),
        name="select",
    )(s4)


def _peer_gates(act, rank2_ref, p2_ref, nsel_ref, p1_ref, gate_ref, slot, cols):
    tp = act.shape[1]
    for il in range(act.shape[0] // N_KEYS):
        x = act[il * N_KEYS:(il + 1) * N_KEYS, :].astype(BF16)
        w = None
        for h in range(PEER_HEADS):
            ns = jnp.broadcast_to(nsel_ref[h, il:il + 1, cols], (N_KEYS, tp)).astype(BF16)
            p1 = jnp.broadcast_to(p1_ref[h, il:il + 1, cols], (N_KEYS, tp)).astype(BF16)
            term = jnp.where(rank2_ref[h, :, cols] <= ns, p2_ref[h, :, cols], 0) * p1
            w = term if w is None else w + term
        inner = x * (GELU_C0 + GELU_C1 * (x * x))
        gate_ref[slot, il * N_KEYS:(il + 1) * N_KEYS, cols] = (x + x * jnp.tanh(inner)) * w


def _peer_kernel(xnt_ref, u_ref, vt_ref, rank2_ref, p2_ref, nsel_ref, p1_ref, x1_ref, fnw_ref,
                 y_ref, acc_ref, gate_ref):
    c = pl.program_id(1)
    last = pl.num_programs(1) - 1
    slot = lax.rem(c, 2)
    tm = xnt_ref.shape[1]
    pieces = [slice(p * PEER_PIECE, (p + 1) * PEER_PIECE) for p in range(tm // PEER_PIECE)]

    def pre_activations():
        return [_dot(u_ref[...], xnt_ref[:, cols]) for cols in pieces]

    def v_product(prev):
        return [_dot(vt_ref[...], gate_ref[prev, :, cols]) for cols in pieces]

    def build(acts, slot):
        for act, cols in zip(acts, pieces):
            _peer_gates(act, rank2_ref, p2_ref, nsel_ref, p1_ref, gate_ref, slot, cols)

    @pl.when(c == 0)
    def _():
        acc_ref[...] = jnp.zeros_like(acc_ref)
        build(pre_activations(), 0)

    @pl.when((c > 0) & (c < last))
    def _():
        acts = pre_activations()
        for upd, cols in zip(v_product(1 - slot), pieces):
            acc_ref[:, cols] += upd
        build(acts, slot)

    @pl.when(c == last)
    def _():
        out = acc_ref[...] + jnp.concatenate(v_product(1 - slot), axis=1)
        y_ref[...] = _rms(x1_ref[...] + out.T, fnw_ref[...])


def _peer(xn_t, u, v_t, rank2, p2, nsel, p1, x1, fnw, tm):
    t = xn_t.shape[1]
    nch, _, ec = v_t.shape
    ni = ec // N_KEYS
    chunk = lambda c, lag: jnp.clip(c - lag, 0, nch - 1)
    tok3 = pl.BlockSpec((PEER_HEADS, N_KEYS, tm), lambda i, c: (0, 0, i))
    chunk3 = pl.BlockSpec((PEER_HEADS, ni, tm), lambda i, c: (0, chunk(c, 0), i))
    return pl.pallas_call(
        _peer_kernel,
        grid=(t // tm, nch + 1),
        in_specs=[pl.BlockSpec((D_MODEL, tm), lambda i, c: (0, i)),
                  pl.BlockSpec((ec, D_MODEL), lambda i, c: (chunk(c, 0), 0)),
                  pl.BlockSpec((None, D_MODEL, ec), lambda i, c: (chunk(c, 1), 0, 0)),
                  tok3, tok3, chunk3, chunk3,
                  pl.BlockSpec((tm, D_MODEL), lambda i, c: (i, 0)),
                  pl.BlockSpec((1, D_MODEL), lambda i, c: (0, 0))],
        out_specs=pl.BlockSpec((tm, D_MODEL), lambda i, c: (i, 0)),
        out_shape=jax.ShapeDtypeStruct((t, D_MODEL), F32),
        scratch_shapes=[pltpu.VMEM((D_MODEL, tm), F32), pltpu.VMEM((2, ec, tm), BF16)],
        compiler_params=pltpu.CompilerParams(
            dimension_semantics=("parallel", "arbitrary"), vmem_limit_bytes=VMEM_LIMIT_BYTES),
        name="peer",
    )(xn_t, u, v_t, rank2, p2, nsel, p1, x1, fnw)


def _fit(n, pref):
    while n % pref:
        pref //= 2
    return pref


def _trunk(x, w):
    bsz, seq, d = x.shape
    assert d == D_MODEL and seq % (2 * POOL_HALO) == 0
    t = bsz * seq
    a, q, kf, gf, kb, gb, v, og, sga, sgb = _inproj(
        x.reshape(t, d), w["norm1"], w["w_in"], w["lbf"], w["lbb"], _fit(t, INPROJ_ROWS))
    r3 = lambda z: z.reshape(bsz, seq, z.shape[-1])
    o_f, o_b = _gla(r3(q), r3(kf), r3(gf), r3(kb), r3(gb), r3(v), _fit(seq, GLA_CHUNK), _fit(bsz, GLA_GROUP))
    x1, xn_t, s_t = _merge(x, r3(a), o_f, o_b, r3(og), r3(sga), r3(sgb), w["merge"], _fit(seq, MERGE_ROWS))
    rank2, p2, nsel, p1 = _select(s_t, _fit(t, PEER_TOKENS))
    y = _peer(xn_t, w["u"], w["v_t"], rank2, p2, nsel, p1, x1.reshape(t, d), w["final_norm"],
              _fit(t, PEER_TOKENS))
    return y.reshape(bsz, seq, d)


def kernel(x_prompt, x_sample, norm1_w, w_in, pool_w, pool_scale, lb_fwd, lb_bwd, hg_norm_w, w_pa, w_pb,
           w_out, norm2_w, peer_wq, peer_keys, peer_u, peer_v, final_norm_w):
    layer = 0
    lower_bound = lambda lb: jnp.cumsum(jax.nn.softmax(lb.astype(F32), axis=0), axis=0)[layer][None, :]
    row = lambda p: p.astype(F32).reshape(1, -1)
    w = {
        "norm1": row(norm1_w[layer]),
        "w_in": w_in[layer].astype(BF16),
        "lbf": lower_bound(lb_fwd),
        "lbb": lower_bound(lb_bwd),
        "merge": (pool_w[layer].astype(BF16), row(pool_scale[layer]), row(hg_norm_w[layer]),
                  w_pa[layer].astype(BF16), w_pb[layer].astype(BF16), w_out[layer].astype(BF16),
                  row(norm2_w[layer]), peer_wq[layer].astype(BF16),
                  peer_keys[layer].astype(BF16).reshape(2 * PEER_HEADS, N_KEYS, PEER_HALF)),
        "u": peer_u[layer].astype(BF16),
        "v_t": peer_v[layer].astype(BF16).reshape(-1, PEER_CHUNK, D_MODEL).transpose(0, 2, 1),
        "final_norm": row(final_norm_w),
    }
    return _trunk(x_prompt, w), _trunk(x_sample, w)
```
